```python
import jax, jax.numpy as jnp
from jax import lax
import numpy as np

D_MODEL = 2048
BATCH = 4
SEQ = 2048
DEPTH = 1
DEC_BATCH = 32
DEC_SEQ = 4
PAST_LEN = 8192
PAGE_SIZE = 128

SB_HEADS = 16
SB_HEAD_DIM = D_MODEL // 32
SB_WIDTH = SB_HEADS * SB_HEAD_DIM
SB_QBLOCK = 128
SB_BIAS_INIT = -7.0
GLA_HEADS = 4
GLA_V_WIDTH = D_MODEL // 2
GLA_K_WIDTH = GLA_V_WIDTH // 2
GLA_DK = GLA_K_WIDTH // GLA_HEADS
GLA_DV = GLA_V_WIDTH // GLA_HEADS
GLA_GATE_RANK = 16
GLA_TAU = 16.0
GLA_CHUNK = 64
D_FF = ((8 * D_MODEL // 3 + 255) // 256) * 256
CONV_W = 3
N_BRANCH = 2
EPS = 1e-6
IN_SIZES = (SB_WIDTH, SB_WIDTH, SB_WIDTH, GLA_K_WIDTH, GLA_K_WIDTH, GLA_V_WIDTH, GLA_V_WIDTH, GLA_GATE_RANK, D_MODEL, D_MODEL)
N_IN = sum(IN_SIZES)

kernel_name = 'hybrid_stickbreak_gla_convffn_adaln_step'


def _rmsnorm(x, g):
    xf = x.astype(jnp.float32)
    y = xf * lax.rsqrt(jnp.mean(xf * xf, axis=-1, keepdims=True) + EPS)
    return (y * g.astype(jnp.float32)).astype(x.dtype)


def _split_last(a, sizes):
    parts, off = [], 0
    for s in sizes:
        parts.append(a[..., off:off + s])
        off += s
    return parts


def _sb_block(q, k, v, t_pos, bias):
    z = jnp.einsum('bqhd,bkhd->bhqk', q, k) * (SB_HEAD_DIM ** -0.5) + bias[None, :, None, None]
    s_pos = jnp.arange(k.shape[1])
    mask = (s_pos[None, :] < t_pos[:, None])[None, None]
    log_beta = jax.nn.log_sigmoid(z)
    log_rest = jnp.where(mask, log_beta - z, 0.0)
    after = lax.cumsum(log_rest, axis=3, reverse=True) - log_rest
    a = jnp.where(mask, jnp.exp(log_beta + after), 0.0)
    return jnp.einsum('bhqk,bkhd->bqhd', a, v)


def _sb_attention(q, k, v, bias):
    q, k, v = q.astype(jnp.float32), k.astype(jnp.float32), v.astype(jnp.float32)
    bias = bias.astype(jnp.float32)
    b, tq, h, d = q.shape
    tk = k.shape[1]
    qb = min(SB_QBLOCK, tq)
    nb = -(-tq // qb)
    qp = jnp.pad(q, ((0, 0), (0, nb * qb - tq), (0, 0), (0, 0)))
    q_blocks = qp.reshape(b, nb, qb, h, d).transpose(1, 0, 2, 3, 4)
    pos_blocks = ((tk - tq) + jnp.arange(nb * qb)).reshape(nb, qb)
    out = lax.map(lambda blk: _sb_block(blk[0], k, v, blk[1], bias), (q_blocks, pos_blocks))
    return out.transpose(1, 0, 2, 3, 4).reshape(b, nb * qb, h, d)[:, :tq]


def _gla_chunk(state, inp):
    q, k, v, g = inp
    c = q.shape[2]
    b = jnp.cumsum(g, axis=2)
    causal = jnp.tril(jnp.ones((c, c), dtype=bool))[None, None, :, :, None]
    diff = b[:, :, :, None, :] - b[:, :, None, :, :]
    decay = jnp.where(causal, jnp.exp(jnp.where(causal, diff, 0.0)), 0.0)
    scores = jnp.einsum('bhtk,bhsk,bhtsk->bhts', q, k, decay)
    out = jnp.einsum('bhts,bhsv->bhtv', scores, v) + jnp.einsum('bhtk,bhkv->bhtv', q * jnp.exp(b), state)
    b_last = b[:, :, -1:, :]
    new_state = (jnp.exp(b_last[:, :, 0, :])[..., None] * state
                 + jnp.einsum('bhsk,bhsv->bhkv', k * jnp.exp(b_last - b), v))
    return new_state, out


def _gla(q, k, v, g, s0):
    bsz, L = q.shape[:2]
    c = min(GLA_CHUNK, L)
    n = -(-L // c)

    def to_chunks(a):
        a = jnp.pad(a.astype(jnp.float32), ((0, 0), (0, n * c - L), (0, 0), (0, 0)))
        return a.reshape(bsz, n, c, a.shape[2], a.shape[3]).transpose(1, 0, 3, 2, 4)

    s, o = lax.scan(_gla_chunk, s0.astype(jnp.float32), (to_chunks(q), to_chunks(k), to_chunks(v), to_chunks(g)))
    o = o.transpose(1, 0, 3, 2, 4).reshape(bsz, n * c, GLA_HEADS, GLA_DV)[:, :L]
    return o, s


def _layer(x, c, past_k, past_v, gla_s0, conv_buf0,
           norm1_g, norm2_g, w_mod, b_mod, w_in, q_norm_g, k_norm_g, sb_bias, w_gk_up, b_gk, gla_onorm_g,
           w_br_sb, w_br_gla, w_out, w_up, conv_w, conv_b, w_down):
    bsz, L, _ = x.shape
    mod = jax.nn.silu(c) @ w_mod + b_mod
    sh1, sc1, gt1, sh2, sc2, gt2 = jnp.split(mod[:, None, :], 6, axis=-1)
    h = _rmsnorm(x, norm1_g) * (1.0 + sc1) + sh1
    sq, sk, sv, gq, gk, gv, gr, gd, m_sb, m_gla = _split_last(h @ w_in, IN_SIZES)
    sq = _rmsnorm(sq.reshape(bsz, L, SB_HEADS, SB_HEAD_DIM), q_norm_g)
    sk = _rmsnorm(sk.reshape(bsz, L, SB_HEADS, SB_HEAD_DIM), k_norm_g)
    sv = sv.reshape(bsz, L, SB_HEADS, SB_HEAD_DIM)
    keys = jnp.concatenate([past_k.astype(sk.dtype), sk], axis=1)
    vals = jnp.concatenate([past_v.astype(sv.dtype), sv], axis=1)
    o_sb = _sb_attention(sq, keys, vals, sb_bias).reshape(bsz, L, SB_WIDTH).astype(x.dtype)
    log_a = jax.nn.log_sigmoid((gd @ w_gk_up + b_gk).astype(jnp.float32)) / GLA_TAU
    to_heads = lambda a: a.reshape(bsz, L, GLA_HEADS, GLA_DK)
    o_gla, s_new = _gla(to_heads(gq) * (GLA_DK ** -0.5), to_heads(gk),
                        gv.reshape(bsz, L, GLA_HEADS, GLA_DV), to_heads(log_a), gla_s0)
    o_gla = _rmsnorm(o_gla.astype(x.dtype), gla_onorm_g).reshape(bsz, L, GLA_V_WIDTH) * jax.nn.silu(gr)
    mixed = jax.nn.sigmoid(m_sb) * (o_sb @ w_br_sb) + jax.nn.sigmoid(m_gla) * (o_gla @ w_br_gla)
    x = x + gt1 * (mixed @ w_out)
    h2 = _rmsnorm(x, norm2_g) * (1.0 + sc2) + sh2
    u_gate, u_val = jnp.split(h2 @ w_up, 2, axis=-1)
    padded = jnp.concatenate([conv_buf0.astype(u_gate.dtype), u_gate], axis=1)
    conv = conv_b + sum(conv_w[i] * padded[:, i:i + L] for i in range(CONV_W))
    x = x + gt2 * ((jax.nn.silu(conv) * u_val) @ w_down)
    return x, sk, sv, s_new.astype(x.dtype), padded[:, L:]


def setup_inputs(seed: int = 0) -> dict:
    key = jax.random.key(seed)
    ks = jax.random.split(key, 32)
    n_pages = PAST_LEN // PAGE_SIZE
    n_used = DEC_BATCH * n_pages
    n_pool = n_used + max(1, n_used // 4)
    nrm = lambda k, shape, scale: scale * jax.random.normal(k, shape, jnp.float32)
    page_table = jax.random.permutation(ks[6], n_pool)[:n_used].reshape(DEC_BATCH, n_pages).astype(jnp.int32)
    return {
        'x_prompt': nrm(ks[0], (BATCH, SEQ, D_MODEL), 1.0),
        'x_sample': nrm(ks[1], (DEC_BATCH, DEC_SEQ, D_MODEL), 1.0),
        'c_prompt': nrm(ks[2], (BATCH, D_MODEL), 1.0),
        'c_sample': nrm(ks[3], (DEC_BATCH, D_MODEL), 1.0),
        'cache_k_sb': nrm(ks[4], (DEPTH, n_pool, PAGE_SIZE, SB_HEADS, SB_HEAD_DIM), 1.0),
        'cache_v_sb': nrm(ks[5], (DEPTH, n_pool, PAGE_SIZE, SB_HEADS, SB_HEAD_DIM), 1.0),
        'page_table': page_table,
        'state_gla': nrm(ks[7], (DEPTH, DEC_BATCH, GLA_HEADS, GLA_DK, GLA_DV), 1.0),
        'state_ffn_conv': nrm(ks[8], (DEPTH, DEC_BATCH, CONV_W - 1, D_FF), 1.0),
        'norm1_g': 1.0 + nrm(ks[9], (DEPTH, D_MODEL), 0.02),
        'norm2_g': 1.0 + nrm(ks[10], (DEPTH, D_MODEL), 0.02),
        'w_mod': nrm(ks[11], (DEPTH, D_MODEL, 6 * D_MODEL), 0.5 * D_MODEL ** -0.5),
        'b_mod': nrm(ks[12], (DEPTH, 6 * D_MODEL), 0.02),
        'w_in': nrm(ks[13], (DEPTH, D_MODEL, N_IN), D_MODEL ** -0.5),
        'q_norm_g': 1.0 + nrm(ks[14], (DEPTH, SB_HEAD_DIM), 0.02),
        'k_norm_g': 1.0 + nrm(ks[15], (DEPTH, SB_HEAD_DIM), 0.02),
        'sb_bias': SB_BIAS_INIT + nrm(ks[26], (DEPTH, SB_HEADS), 0.1),
        'w_gk_up': nrm(ks[16], (DEPTH, GLA_GATE_RANK, GLA_K_WIDTH), GLA_GATE_RANK ** -0.5),
        'b_gk': nrm(ks[17], (DEPTH, GLA_K_WIDTH), 0.02),
        'gla_onorm_g': 1.0 + nrm(ks[18], (DEPTH, GLA_DV), 0.02),
        'w_br_sb': nrm(ks[19], (DEPTH, SB_WIDTH, D_MODEL), SB_WIDTH ** -0.5),
        'w_br_gla': nrm(ks[20], (DEPTH, GLA_V_WIDTH, D_MODEL), GLA_V_WIDTH ** -0.5),
        'w_out': nrm(ks[21], (DEPTH, D_MODEL, D_MODEL), D_MODEL ** -0.5),
        'w_up': nrm(ks[22], (DEPTH, D_MODEL, 2 * D_FF), D_MODEL ** -0.5),
        'conv_w': nrm(ks[23], (DEPTH, CONV_W, D_FF), CONV_W ** -0.5),
        'conv_b': nrm(ks[24], (DEPTH, D_FF), 0.02),
        'w_down': nrm(ks[25], (DEPTH, D_FF, D_MODEL), D_FF ** -0.5),
    }


def reference(x_prompt, x_sample, c_prompt, c_sample, cache_k_sb, cache_v_sb, page_table, state_gla, state_ffn_conv,
              norm1_g, norm2_g, w_mod, b_mod, w_in, q_norm_g, k_norm_g, sb_bias, w_gk_up, b_gk, gla_onorm_g,
              w_br_sb, w_br_gla, w_out, w_up, conv_w, conv_b, w_down):
    xp, xs = x_prompt, x_sample
    bp = xp.shape[0]
    bs = xs.shape[0]
    kp_l, vp_l, sp_l, cp_l, ks_l, vs_l, ss_l, cs_l = [], [], [], [], [], [], [], []
    for l in range(DEPTH):
        weights = (norm1_g[l], norm2_g[l], w_mod[l], b_mod[l], w_in[l], q_norm_g[l], k_norm_g[l], sb_bias[l],
                   w_gk_up[l], b_gk[l], gla_onorm_g[l], w_br_sb[l], w_br_gla[l], w_out[l], w_up[l],
                   conv_w[l], conv_b[l], w_down[l])
        empty = jnp.zeros((bp, 0, SB_HEADS, SB_HEAD_DIM), xp.dtype)
        xp, kp, vp, sp, cp = _layer(xp, c_prompt, empty, empty,
                                    jnp.zeros((bp, GLA_HEADS, GLA_DK, GLA_DV), jnp.float32),
                                    jnp.zeros((bp, CONV_W - 1, D_FF), xp.dtype), *weights)
        past_k = cache_k_sb[l][page_table].reshape(bs, -1, SB_HEADS, SB_HEAD_DIM)
        past_v = cache_v_sb[l][page_table].reshape(bs, -1, SB_HEADS, SB_HEAD_DIM)
        xs, ks, vs, ss, cs = _layer(xs, c_sample, past_k, past_v, state_gla[l], state_ffn_conv[l], *weights)
        kp_l.append(kp); vp_l.append(vp); sp_l.append(sp); cp_l.append(cp)
        ks_l.append(ks); vs_l.append(vs); ss_l.append(ss); cs_l.append(cs)
    return (xp, xs, jnp.stack(kp_l), jnp.stack(vp_l), jnp.stack(sp_l), jnp.stack(cp_l),
            jnp.stack(ks_l), jnp.stack(vs_l), jnp.stack(ss_l), jnp.stack(cs_l))
```

```python
import functools
import math
from typing import NamedTuple

import jax
import jax.numpy as jnp
import numpy as np
from jax import lax
from jax.experimental import pallas as pl
from jax.experimental.pallas import tpu as pltpu

F32 = jnp.float32
BF16 = jnp.bfloat16
EPS = 1e-6
GLA_TAU = 16.0
VMEM_LIMIT_BYTES = 48 * 1024 * 1024
LANES = 128
NT_DIMS = (((1,), (1,)), ((), ()))
TN_DIMS = (((0,), (0,)), ((), ()))


def _params(*sem):
    return pltpu.CompilerParams(dimension_semantics=sem, vmem_limit_bytes=VMEM_LIMIT_BYTES)


def _dot(a, b):
    return jnp.dot(a, b, preferred_element_type=F32)


def _softplus_parts(z):
    sp = jnp.log1p(jnp.exp(-jnp.abs(z)))
    return jnp.minimum(z, 0.0) - sp, -(jnp.maximum(z, 0.0) + sp)


def _silu(x):
    return x * jax.nn.sigmoid(x)


class _Rows(NamedTuple):
    n_seq: int
    seq_len: int
    tm: int

    @property
    def total(self):
        return self.n_seq * self.seq_len

    @property
    def per_row(self):
        return self.seq_len < self.tm


def _seq_vec(rows, v, row_tile_of):
    d = v.shape[1]
    if rows.per_row:
        return jnp.repeat(v, rows.seq_len, axis=0), pl.BlockSpec((rows.tm, d), lambda *g: (row_tile_of(*g), 0))
    tiles_per_seq = rows.seq_len // rows.tm
    return v[:, None, :], pl.BlockSpec((None, 1, d), lambda *g: (row_tile_of(*g) // tiles_per_seq, 0, 0))


def _mod_kernel(c_ref, w_ref, b_ref, o_ref):
    c = c_ref[...]
    o_ref[...] = _dot(_silu(c).astype(BF16), w_ref[...].astype(BF16)) + b_ref[...]


def _mod_call(c, w_mod, b_mod):
    r, d = c.shape
    n = w_mod.shape[1]
    tn = 512
    return pl.pallas_call(
        _mod_kernel,
        grid=(n // tn,),
        in_specs=[pl.BlockSpec((r, d), lambda j: (0, 0)),
                  pl.BlockSpec((d, tn), lambda j: (0, j)),
                  pl.BlockSpec((1, tn), lambda j: (0, j))],
        out_specs=pl.BlockSpec((r, tn), lambda j: (0, j)),
        out_shape=jax.ShapeDtypeStruct((r, n), F32),
        compiler_params=_params("arbitrary"),
        name="mod",
    )(c, w_mod, b_mod.reshape(1, n))


def _rms_modulate(x, g, sc, sh):
    y = x * lax.rsqrt(jnp.mean(x * x, axis=-1, keepdims=True) + EPS) * g
    return y * (1.0 + sc) + sh


def _prenorm_kernel(x_ref, g_ref, sc_ref, sh_ref, o_ref):
    o_ref[...] = _rms_modulate(x_ref[...], g_ref[...], sc_ref[...], sh_ref[...]).astype(o_ref.dtype)


def _prenorm_call(rows, x, g, sc, sh):
    t, d = x.shape
    sc_a, sc_s = _seq_vec(rows, sc, lambda i: i)
    sh_a, sh_s = _seq_vec(rows, sh, lambda i: i)
    return pl.pallas_call(
        _prenorm_kernel,
        grid=(t // rows.tm,),
        in_specs=[pl.BlockSpec((rows.tm, d), lambda i: (i, 0)),
                  pl.BlockSpec((1, d), lambda i: (0, 0)), sc_s, sh_s],
        out_specs=pl.BlockSpec((rows.tm, d), lambda i: (i, 0)),
        out_shape=jax.ShapeDtypeStruct((t, d), BF16),
        compiler_params=_params("arbitrary"),
        name="prenorm",
    )(x, g.reshape(1, d), sc_a, sh_a)


def _proj_kernel(h_ref, w_ref, *rest, head_dim, use_col):
    idx = 0
    col_ref = p_ref = None
    if use_col:
        col_ref = rest[idx]
        idx += 1
    if head_dim:
        p_ref = rest[idx]
        idx += 1
    o_refs = rest[idx:]
    acc = _dot(h_ref[...], w_ref[...])
    if head_dim:
        pw = p_ref.shape[0]
        parts = []
        for s in range(acc.shape[1] // pw):
            a = acc[:, s * pw:(s + 1) * pw]
            ss = _dot((a * a).astype(BF16), p_ref[...])
            parts.append(a * lax.rsqrt(ss * (1.0 / head_dim) + EPS))
        acc = parts[0] if len(parts) == 1 else jnp.concatenate(parts, axis=1)
    if use_col:
        acc = acc * col_ref[...]
    for o_ref in o_refs:
        o_ref[...] = acc.astype(o_ref.dtype)


def _proj_call(rows, h, w, out_dtypes, col=None, head_dim=0, tn=512, name="proj"):
    t, d = h.shape
    n = w.shape[1]
    tn = min(tn, n)
    tm = rows.tm
    ins = [h, w]
    specs = [pl.BlockSpec((tm, d), lambda i, j: (i, 0)), pl.BlockSpec((d, tn), lambda i, j: (0, j))]
    if col is not None:
        ins.append(col.reshape(1, n).astype(F32))
        specs.append(pl.BlockSpec((1, tn), lambda i, j: (0, j)))
    if head_dim:
        pw = min(256, tn)
        blk = np.arange(pw) // head_dim
        ins.append(jnp.asarray(blk[:, None] == blk[None, :], BF16))
        specs.append(pl.BlockSpec((pw, pw), lambda i, j: (0, 0)))
    outs = pl.pallas_call(
        functools.partial(_proj_kernel, head_dim=head_dim, use_col=col is not None),
        grid=(t // tm, n // tn),
        in_specs=specs,
        out_specs=[pl.BlockSpec((tm, tn), lambda i, j: (i, j)) for _ in out_dtypes],
        out_shape=[jax.ShapeDtypeStruct((t, n), dt) for dt in out_dtypes],
        compiler_params=_params("arbitrary", "arbitrary"),
        name=name,
    )(*ins)
    return outs


def _sb_prompt_kernel(bias_ref, q_ref, k_ref, v_ref, u_ref, o_ref, *, tq, hd):
    pair = pl.program_id(1)
    i = pl.program_id(2)
    q = q_ref[...]
    lane = lax.broadcasted_iota(jnp.int32, q.shape, 1)
    zero = jnp.zeros_like(q)
    q_heads = (jnp.where(lane < hd, q, zero), jnp.where(lane >= hd, q, zero))
    biases = (bias_ref[2 * pair], bias_ref[2 * pair + 1])
    u = u_ref[...]
    row = lax.broadcasted_iota(jnp.int32, (tq, tq), 0)
    col = lax.broadcasted_iota(jnp.int32, (tq, tq), 1)
    causal = col < row
    out_lane = lax.broadcasted_iota(jnp.int32, (tq, 2 * hd), 1)

    def block(j, carry, masked):
        acc, c0, c1 = carry
        off = pl.multiple_of(j * tq, tq)
        kb = k_ref[pl.ds(off, tq), :]
        vb = v_ref[pl.ds(off, tq), :]
        pvs, cs = [], []
        for qh, bias, c in zip(q_heads, biases, (c0, c1)):
            z = lax.dot_general(qh, kb, NT_DIMS, preferred_element_type=F32) + bias
            lb, lr = _softplus_parts(z)
            if masked:
                lr = jnp.where(causal, lr, 0.0)
            after = _dot(lr.astype(BF16), u)
            a = jnp.exp(lb + after + c)
            if masked:
                a = jnp.where(causal, a, 0.0)
            pvs.append(_dot(a.astype(BF16), vb))
            cs.append(c + jnp.sum(lr, axis=-1, keepdims=True))
        acc = acc + jnp.where(out_lane < hd, pvs[0], pvs[1])
        return acc, cs[0], cs[1]

    init = (jnp.zeros((tq, 2 * hd), F32), jnp.zeros((tq, 1), F32), jnp.zeros((tq, 1), F32))
    carry = block(i, init, True)
    carry = lax.fori_loop(0, i, lambda s, cr: block(i - 1 - s, cr, False), carry)
    o_ref[...] = carry[0].astype(o_ref.dtype)


def _sb_prompt_call(q, k, v, bias, n_seq, seq_len, hd, tq=256):
    t, width = q.shape
    n_pairs = width // (2 * hd)
    nq = seq_len // tq
    u = jnp.asarray(np.tril(np.ones((tq, tq), np.float32), -1), BF16)
    return pl.pallas_call(
        functools.partial(_sb_prompt_kernel, tq=tq, hd=hd),
        grid=(n_seq, n_pairs, nq),
        in_specs=[pl.BlockSpec(memory_space=pltpu.SMEM),
                  pl.BlockSpec((tq, 2 * hd), lambda b, p, i: (b * nq + i, p)),
                  pl.BlockSpec((seq_len, 2 * hd), lambda b, p, i: (b, p)),
                  pl.BlockSpec((seq_len, 2 * hd), lambda b, p, i: (b, p)),
                  pl.BlockSpec((tq, tq), lambda b, p, i: (0, 0))],
        out_specs=pl.BlockSpec((tq, 2 * hd), lambda b, p, i: (b * nq + i, p)),
        out_shape=jax.ShapeDtypeStruct((t, width), BF16),
        compiler_params=_params("arbitrary", "arbitrary", "arbitrary"),
        name="sb_prompt",
    )(bias, q, k, v, u)


def _sb_sample_kernel(pt_ref, qbd_ref, bias_ref, kn_ref, vn_ref, *rest, n_slots, n_heads, hd, page):
    del pt_ref
    kp_refs = rest[:n_slots]
    vp_refs = rest[n_slots:2 * n_slots]
    u_ref, o_ref, acc_ref, c_ref = rest[2 * n_slots:]
    s = pl.program_id(1)
    qbd = qbd_ref[...]
    n_rows, width = qbd.shape
    n_new = kn_ref.shape[0]

    def block(kb, vb, mask):
        z = lax.dot_general(qbd, kb, NT_DIMS, preferred_element_type=F32) + bias_ref[...]
        lb, lr = _softplus_parts(z)
        if mask is not None:
            lr = jnp.where(mask, lr, 0.0)
        after = _dot(lr.astype(BF16), u_ref[...])
        a = jnp.exp(lb + after + c_ref[...])
        if mask is not None:
            a = jnp.where(mask, a, 0.0)
        acc_ref[...] += _dot(a.astype(BF16), vb)
        c_ref[...] += jnp.sum(lr, axis=-1, keepdims=True)

    @pl.when(s == 0)
    def _():
        acc_ref[...] = jnp.zeros_like(acc_ref)
        c_ref[...] = jnp.zeros_like(c_ref)
        pad = jnp.zeros((page - n_new, width), F32)
        kb = jnp.concatenate([kn_ref[...], pad], axis=0).astype(BF16)
        vb = jnp.concatenate([vn_ref[...], pad], axis=0).astype(BF16)
        row = lax.broadcasted_iota(jnp.int32, (n_rows, page), 0)
        col = lax.broadcasted_iota(jnp.int32, (n_rows, page), 1)
        block(kb, vb, col * n_heads < row - row % n_heads)

    @pl.when(s > 0)
    def _():
        for slot in range(n_slots):
            block(kp_refs[slot][...].astype(BF16), vp_refs[slot][...].astype(BF16), None)

    @pl.when(s == pl.num_programs(1) - 1)
    def _():
        row = lax.broadcasted_iota(jnp.int32, (n_rows, width), 0)
        col = lax.broadcasted_iota(jnp.int32, (n_rows, width), 1)
        own = jnp.where(col // hd == row % n_heads, acc_ref[...], 0.0)
        o_ref[...] = jnp.sum(own.reshape(n_rows // n_heads, n_heads, width), axis=1).astype(o_ref.dtype)


def _sb_sample_call(q, k_new, v_new, cache_k, cache_v, page_table, bias, n_heads, hd, n_slots=4):
    bsz, n_q, width = q.shape
    page = cache_k.shape[1]
    n_pages = page_table.shape[1]
    n_steps = n_pages // n_slots
    n_rows = n_q * n_heads
    n_new = 8
    eye = jnp.eye(n_heads, dtype=q.dtype)
    qbd = jnp.einsum("bqhd,hg->bqhgd", q.reshape(bsz, n_q, n_heads, hd), eye).reshape(bsz, n_rows, width)
    bias_rows = jnp.tile(bias.astype(F32), n_q).reshape(n_rows, 1)
    pad = ((0, 0), (0, n_new - n_q), (0, 0))
    k_new, v_new = jnp.pad(k_new, pad), jnp.pad(v_new, pad)
    u = jnp.asarray(np.tril(np.ones((page, page), np.float32), -1), BF16)

    def page_spec(slot):
        def imap(b, s, pt):
            return pt[b, n_pages - 1 - (jnp.maximum(s, 1) - 1) * n_slots - slot], 0, 0
        return pl.BlockSpec((None, page, width), imap)

    per_seq = lambda b, s, pt: (b, 0, 0)
    const = lambda b, s, pt: (0, 0)
    grid_spec = pltpu.PrefetchScalarGridSpec(
        num_scalar_prefetch=1,
        grid=(bsz, n_steps + 1),
        in_specs=[pl.BlockSpec((None, n_rows, width), per_seq),
                  pl.BlockSpec((n_rows, 1), const),
                  pl.BlockSpec((None, n_new, width), per_seq),
                  pl.BlockSpec((None, n_new, width), per_seq)]
                 + [page_spec(slot) for slot in range(n_slots)] * 2
                 + [pl.BlockSpec((page, page), const)],
        out_specs=pl.BlockSpec((None, n_q, width), per_seq),
        scratch_shapes=[pltpu.VMEM((n_rows, width), F32), pltpu.VMEM((n_rows, LANES), F32)],
    )
    return pl.pallas_call(
        functools.partial(_sb_sample_kernel, n_slots=n_slots, n_heads=n_heads, hd=hd, page=page),
        grid_spec=grid_spec,
        out_shape=jax.ShapeDtypeStruct((bsz, n_q, width), BF16),
        compiler_params=_params("arbitrary", "arbitrary"),
        name="sb_sample",
    )(page_table, qbd, bias_rows, k_new, v_new, *([cache_k] * n_slots), *([cache_v] * n_slots), u)


def _gla_tables(c):
    n_lev = int(math.log2(c))
    assert 1 << n_lev == c
    m = np.zeros((n_lev + 2, c, c), np.float32)
    level = np.full((c, c), -1, np.int32)
    idx = np.arange(c)
    for l in range(n_lev):
        half = c >> (l + 1)
        start = idx // (2 * half) * (2 * half)
        bound = start + half
        for t in range(c):
            if t >= bound[t]:
                m[l, t, bound[t]:t + 1] = 1.0
            else:
                m[l, t, t + 1:bound[t]] = 1.0
        same = start[:, None] == start[None, :]
        level[same & (idx[:, None] >= bound[:, None]) & (idx[None, :] < bound[None, :])] = l
    level[idx, idx] = n_lev
    for t in range(c):
        m[n_lev, t, :t + 1] = 1.0
        m[n_lev + 1, t, t + 1:] = 1.0
    return m.reshape((n_lev + 2) * c, c), level, n_lev


def _split3(x):
    hi = x.astype(BF16)
    r = x - hi.astype(F32)
    mid = r.astype(BF16)
    lo = (r - mid.astype(F32)).astype(BF16)
    return hi, mid, lo


def _gla_kernel(q_ref, k_ref, v_ref, gr_ref, gd_ref, wup_ref, bgk_ref, gno_ref, s0_ref, m_ref, lvl_ref,
                o_ref, sout_ref, s_ref, *, chunk, n_lev, rank, valid):
    rb = pl.program_id(2)

    @pl.when(rb == 0)
    def _():
        s_ref[...] = s0_ref[...]

    n_rows, dk = q_ref.shape
    level = lvl_ref[...]
    m = m_ref[...]
    ones = jnp.ones((chunk, dk), BF16)
    for ci in range(n_rows // chunk):
        sl = slice(ci * chunk, (ci + 1) * chunk)
        q, k, v = q_ref[sl, :], k_ref[sl, :], v_ref[sl, :]
        z = _dot(gd_ref[sl, :rank].astype(BF16), wup_ref[...]) + bgk_ref[...]
        g = _softplus_parts(z)[0] * (1.0 / GLA_TAU)
        if valid < chunk:
            g = jnp.where(lax.broadcasted_iota(jnp.int32, g.shape, 0) < valid, g, 0.0)
        g3 = _split3(g)
        x = jnp.exp(sum(_dot(m, gp) for gp in g3))
        scores = jnp.where(level == n_lev,
                           lax.dot_general(q.astype(BF16), k.astype(BF16), NT_DIMS, preferred_element_type=F32), 0.0)
        for l in range(n_lev):
            xl = x[l * chunk:(l + 1) * chunk]
            s_l = lax.dot_general((q * xl).astype(BF16), (k * xl).astype(BF16), NT_DIMS, preferred_element_type=F32)
            scores = jnp.where(level == l, s_l, scores)
        qb = (q * x[n_lev * chunk:(n_lev + 1) * chunk]).astype(BF16)
        kb = (k * x[(n_lev + 1) * chunk:]).astype(BF16)
        state = s_ref[...]
        o = _dot(scores.astype(BF16), v) + _dot(qb, state.astype(BF16))
        d_col = jnp.exp(sum(lax.dot_general(gp, ones, TN_DIMS, preferred_element_type=F32) for gp in g3))
        d_full = jnp.concatenate([d_col] * (state.shape[1] // dk), axis=1)
        s_ref[...] = state * d_full + lax.dot_general(kb, v, TN_DIMS, preferred_element_type=F32)
        on = o * lax.rsqrt(jnp.mean(o * o, axis=-1, keepdims=True) + EPS) * gno_ref[...]
        o_ref[sl, :] = (on * _silu(gr_ref[sl, :])).astype(o_ref.dtype)

    @pl.when(rb == pl.num_programs(2) - 1)
    def _():
        sout_ref[...] = s_ref[...]


def _gla_call(q, k, v, gr, gd, w_gk_up, b_gk, gno, s0, chunk, valid, rows_per_step):
    bsz, seq, kw = q.shape
    n_heads, dk, dv = s0.shape[1:]
    rank = w_gk_up.shape[0]
    m, level, n_lev = _gla_tables(chunk)
    r = rows_per_step
    blk = lambda w: pl.BlockSpec((None, r, w), lambda b, h, i: (b, i, h))
    const2 = lambda b, h, i: (0, 0)
    state_spec = pl.BlockSpec((None, None, dk, dv), lambda b, h, i: (b, h, 0, 0))
    return pl.pallas_call(
        functools.partial(_gla_kernel, chunk=chunk, n_lev=n_lev, rank=rank, valid=valid),
        grid=(bsz, n_heads, seq // r),
        in_specs=[blk(dk), blk(dk), blk(dv), blk(dv),
                  pl.BlockSpec((None, r, gd.shape[2]), lambda b, h, i: (b, i, 0)),
                  pl.BlockSpec((rank, dk), lambda b, h, i: (0, h)),
                  pl.BlockSpec((1, dk), lambda b, h, i: (0, h)),
                  pl.BlockSpec((1, dv), const2),
                  state_spec,
                  pl.BlockSpec(m.shape, const2),
                  pl.BlockSpec(level.shape, const2)],
        out_specs=[blk(dv), state_spec],
        out_shape=[jax.ShapeDtypeStruct((bsz, seq, n_heads * dv), BF16),
                   jax.ShapeDtypeStruct(s0.shape, F32)],
        scratch_shapes=[pltpu.VMEM((dk, dv), F32)],
        compiler_params=_params("arbitrary", "arbitrary", "arbitrary"),
        name="gla",
    )(q, k, v, gr, gd, w_gk_up.astype(BF16), b_gk.reshape(1, -1), gno.reshape(1, dv), s0,
      jnp.asarray(m, BF16), jnp.asarray(level))


def _merge_kernel(h_ref, osb_ref, ogla_ref, wmsb_ref, wmgla_ref, wbsb_ref, wbgla_ref, o_ref):
    h = h_ref[...]
    gate_sb = jax.nn.sigmoid(_dot(h, wmsb_ref[...]))
    gate_gla = jax.nn.sigmoid(_dot(h, wmgla_ref[...]))
    mixed = gate_sb * _dot(osb_ref[...], wbsb_ref[...]) + gate_gla * _dot(ogla_ref[...], wbgla_ref[...])
    o_ref[...] = mixed.astype(o_ref.dtype)


def _merge_call(rows, h, o_sb, o_gla, w_msb, w_mgla, w_br_sb, w_br_gla, tn=512):
    t, d = h.shape
    tm = min(rows.tm, 512)
    row = lambda w: pl.BlockSpec((tm, w), lambda i, j: (i, 0))
    wcol = lambda kdim: pl.BlockSpec((kdim, tn), lambda i, j: (0, j))
    return pl.pallas_call(
        _merge_kernel,
        grid=(t // tm, d // tn),
        in_specs=[row(d), row(o_sb.shape[1]), row(o_gla.shape[1]),
                  wcol(d), wcol(d), wcol(o_sb.shape[1]), wcol(o_gla.shape[1])],
        out_specs=pl.BlockSpec((tm, tn), lambda i, j: (i, j)),
        out_shape=jax.ShapeDtypeStruct((t, d), BF16),
        compiler_params=_params("arbitrary", "arbitrary"),
        name="merge",
    )(h, o_sb, o_gla, w_msb, w_mgla, w_br_sb, w_br_gla)


def _outproj_kernel(mixed_ref, x_ref, w_ref, gt_ref, g_ref, sc_ref, sh_ref, x1_ref, h2_ref):
    x1 = x_ref[...] + gt_ref[...] * _dot(mixed_ref[...], w_ref[...])
    x1_ref[...] = x1
    h2_ref[...] = _rms_modulate(x1, g_ref[...], sc_ref[...], sh_ref[...]).astype(h2_ref.dtype)


def _outproj_call(rows, mixed, x, w_out, gt, g2, sc, sh):
    t, d = x.shape
    rows = rows._replace(tm=min(rows.tm, 256))
    tm = rows.tm
    vecs = [_seq_vec(rows, v, lambda i: i) for v in (gt, sc, sh)]
    row = pl.BlockSpec((tm, d), lambda i: (i, 0))
    return pl.pallas_call(
        _outproj_kernel,
        grid=(t // tm,),
        in_specs=[row, row, pl.BlockSpec((d, d), lambda i: (0, 0)), vecs[0][1],
                  pl.BlockSpec((1, d), lambda i: (0, 0)), vecs[1][1], vecs[2][1]],
        out_specs=[row, row],
        out_shape=[jax.ShapeDtypeStruct((t, d), F32), jax.ShapeDtypeStruct((t, d), BF16)],
        compiler_params=_params("arbitrary"),
        name="outproj",
    )(mixed, x, w_out, vecs[0][0], g2.reshape(1, d), vecs[1][0], vecs[2][0])


CONV_TAIL = 8


def _conv_gate(a, a1, a2, u, cw_ref, cb_ref):
    conv = cb_ref[...] + cw_ref[0:1, :] * a2 + cw_ref[1:2, :] * a1 + cw_ref[2:3, :] * a
    return _silu(conv) * u


def _up_carry_kernel(h2_ref, wg_ref, wv_ref, cw_ref, cb_ref, g_ref, tail_ref, carry_ref, *, tiles_per_seq):
    i = pl.program_id(1)

    @pl.when(i % tiles_per_seq == 0)
    def _():
        carry_ref[...] = jnp.zeros_like(carry_ref)

    h2 = h2_ref[...]
    a = _dot(h2, wg_ref[...])
    u = _dot(h2, wv_ref[...])
    tm = a.shape[0]
    prev = carry_ref[...]
    head_row = lax.broadcasted_iota(jnp.int32, prev.shape, 0)
    shifted = []
    for shift in (1, 2):
        r = pltpu.roll(a, shift, 0)
        top = jnp.where(head_row < shift, pltpu.roll(prev, shift, 0), r[:CONV_TAIL])
        shifted.append(jnp.concatenate([top, r[CONV_TAIL:]], axis=0))
    tail = a[tm - CONV_TAIL:]
    carry_ref[...] = tail
    tail_ref[...] = tail
    g_ref[...] = _conv_gate(a, shifted[0], shifted[1], u, cw_ref, cb_ref).astype(g_ref.dtype)


def _up_state_kernel(h2_ref, wg_ref, wv_ref, cw_ref, cb_ref, e1_ref, e2_ref, g_ref, a_ref, *, seq_len):
    h2 = h2_ref[...]
    a = _dot(h2, wg_ref[...])
    u = _dot(h2, wv_ref[...])
    pos = lax.broadcasted_iota(jnp.int32, a.shape, 0) % seq_len
    a1 = jnp.where(pos < 1, e1_ref[...], pltpu.roll(a, 1, 0))
    a2 = jnp.where(pos < 2, e2_ref[...], pltpu.roll(a, 2, 0))
    a_ref[...] = a
    g_ref[...] = _conv_gate(a, a1, a2, u, cw_ref, cb_ref).astype(g_ref.dtype)


def _up_call(rows, h2, w_up, conv_w, conv_b, conv_state, tn=512):
    t, d = h2.shape
    f = w_up.shape[1] // 2
    n_col = f // tn
    tm = min(rows.tm, 512)
    common_in = [pl.BlockSpec((tm, d), lambda j, i: (i, 0)),
                 pl.BlockSpec((d, tn), lambda j, i: (0, j)),
                 pl.BlockSpec((d, tn), lambda j, i: (0, j + n_col)),
                 pl.BlockSpec(conv_w.shape[:1] + (tn,), lambda j, i: (0, j)),
                 pl.BlockSpec((1, tn), lambda j, i: (0, j))]
    tile = pl.BlockSpec((tm, tn), lambda j, i: (i, j))
    g_shape = jax.ShapeDtypeStruct((t, f), BF16)
    args = (h2, w_up, w_up, conv_w, conv_b.reshape(1, f))
    if rows.seq_len >= tm:
        tiles_per_seq = rows.seq_len // tm
        g, tail = pl.pallas_call(
            functools.partial(_up_carry_kernel, tiles_per_seq=tiles_per_seq),
            grid=(n_col, t // tm),
            in_specs=common_in,
            out_specs=[tile, pl.BlockSpec((None, CONV_TAIL, tn), lambda j, i: (i // tiles_per_seq, 0, j))],
            out_shape=[g_shape, jax.ShapeDtypeStruct((rows.n_seq, CONV_TAIL, f), F32)],
            scratch_shapes=[pltpu.VMEM((CONV_TAIL, tn), F32)],
            compiler_params=_params("arbitrary", "arbitrary"),
            name="up_carry",
        )(*args)
        return g, tail[:, CONV_TAIL - 2:, :]
    n_seq, seq_len = rows.n_seq, rows.seq_len
    zeros = jnp.zeros((n_seq, seq_len - 1, f), F32)
    e1 = jnp.concatenate([conv_state[:, 1:2], zeros], axis=1).reshape(t, f)
    e2 = jnp.concatenate([conv_state[:, 0:2], zeros[:, 1:]], axis=1).reshape(t, f)
    g, a = pl.pallas_call(
        functools.partial(_up_state_kernel, seq_len=seq_len),
        grid=(n_col, t // tm),
        in_specs=common_in + [tile, tile],
        out_specs=[tile, tile],
        out_shape=[g_shape, jax.ShapeDtypeStruct((t, f), F32)],
        compiler_params=_params("arbitrary", "arbitrary"),
        name="up_state",
    )(*args, e1, e2)
    return g, a.reshape(n_seq, seq_len, f)[:, seq_len - 2:, :]


def _down_kernel(g_ref, w_ref, x_ref, gt_ref, o_ref):
    o_ref[...] = x_ref[...] + gt_ref[...] * _dot(g_ref[...], w_ref[...])


def _down_call(rows, g, w_down, x1, gt, tn=512):
    t, f = g.shape
    d = w_down.shape[1]
    rows = rows._replace(tm=min(rows.tm, 512))
    tm = rows.tm
    gt_a, _ = _seq_vec(rows, gt, lambda i, j: i)
    if rows.per_row:
        gt_s = pl.BlockSpec((tm, tn), lambda i, j: (i, j))
    else:
        tiles_per_seq = rows.seq_len // tm
        gt_s = pl.BlockSpec((None, 1, tn), lambda i, j: (i // tiles_per_seq, 0, j))
    tile = pl.BlockSpec((tm, tn), lambda i, j: (i, j))
    return pl.pallas_call(
        _down_kernel,
        grid=(t // tm, d // tn),
        in_specs=[pl.BlockSpec((tm, f), lambda i, j: (i, 0)),
                  pl.BlockSpec((f, tn), lambda i, j: (0, j)), tile, gt_s],
        out_specs=tile,
        out_shape=jax.ShapeDtypeStruct((t, d), F32),
        compiler_params=_params("arbitrary", "arbitrary"),
        name="down",
    )(g, w_down, x1, gt_a)


class _Weights(NamedTuple):
    norm1_g: jax.Array
    norm2_g: jax.Array
    w_q: jax.Array
    w_k: jax.Array
    w_v: jax.Array
    w_gqk: jax.Array
    w_gv: jax.Array
    w_gr: jax.Array
    w_gd: jax.Array
    w_msb: jax.Array
    w_mgla: jax.Array
    q_col: jax.Array
    k_col: jax.Array
    gqk_col: jax.Array
    sb_bias: jax.Array
    w_gk_up: jax.Array
    b_gk: jax.Array
    gla_onorm_g: jax.Array
    w_br_sb: jax.Array
    w_br_gla: jax.Array
    w_out: jax.Array
    w_up: jax.Array
    conv_w: jax.Array
    conv_b: jax.Array
    w_down: jax.Array


class _Dims(NamedTuple):
    sb_heads: int
    sb_hd: int
    gla_heads: int
    gla_dk: int
    gla_dv: int
    rank: int


def _prep_weights(dims, norm1_g, norm2_g, w_in, q_norm_g, k_norm_g, sb_bias, w_gk_up, b_gk, gla_onorm_g,
                  w_br_sb, w_br_gla, w_out, w_up, conv_w, conv_b, w_down):
    d = w_in.shape[0]
    sbw = dims.sb_heads * dims.sb_hd
    kw = dims.gla_heads * dims.gla_dk
    vw = dims.gla_heads * dims.gla_dv
    sizes = (sbw, sbw, sbw, 2 * kw, vw, vw, dims.rank, d, d)
    offs = np.concatenate([[0], np.cumsum(sizes)])
    w_q, w_k, w_v, w_gqk, w_gv, w_gr, w_gd, w_msb, w_mgla = (
        w_in[:, offs[n]:offs[n + 1]].astype(BF16) for n in range(len(sizes)))
    w_gd = jnp.pad(w_gd, ((0, 0), (0, LANES - dims.rank)))
    return _Weights(
        norm1_g, norm2_g, w_q, w_k, w_v, w_gqk, w_gv, w_gr, w_gd, w_msb, w_mgla,
        q_col=jnp.tile(q_norm_g, dims.sb_heads) * dims.sb_hd ** -0.5,
        k_col=jnp.tile(k_norm_g, dims.sb_heads),
        gqk_col=jnp.concatenate([jnp.full((kw,), dims.gla_dk ** -0.5, F32), jnp.ones((kw,), F32)]),
        sb_bias=sb_bias, w_gk_up=w_gk_up, b_gk=b_gk, gla_onorm_g=gla_onorm_g,
        w_br_sb=w_br_sb.astype(BF16), w_br_gla=w_br_gla.astype(BF16), w_out=w_out.astype(BF16),
        w_up=w_up.astype(BF16), conv_w=conv_w, conv_b=conv_b, w_down=w_down.astype(BF16))


def _layer_group(dims, w, rows, x, mod, gla_s0, conv_state, paged):
    n_seq, seq_len, d = x.shape
    t = n_seq * seq_len
    x2 = x.reshape(t, d)
    sh1, sc1, gt1, sh2, sc2, gt2 = jnp.split(mod, 6, axis=-1)
    h = _prenorm_call(rows, x2, w.norm1_g, sc1, sh1)

    (q,) = _proj_call(rows, h, w.w_q, (BF16,), col=w.q_col, head_dim=dims.sb_hd, name="proj_q")
    k32, k16 = _proj_call(rows, h, w.w_k, (F32, BF16), col=w.k_col, head_dim=dims.sb_hd, name="proj_k")
    v32, v16 = _proj_call(rows, h, w.w_v, (F32, BF16), name="proj_v")
    (gqk,) = _proj_call(rows, h, w.w_gqk, (F32,), col=w.gqk_col, name="proj_gqk")
    (gv,) = _proj_call(rows, h, w.w_gv, (BF16,), name="proj_gv")
    (gr,) = _proj_call(rows, h, w.w_gr, (F32,), name="proj_gr")
    (gd,) = _proj_call(rows, h, w.w_gd, (F32,), name="proj_gd")

    sbw = dims.sb_heads * dims.sb_hd
    if paged is None:
        o_sb = _sb_prompt_call(q, k16, v16, w.sb_bias, n_seq, seq_len, dims.sb_hd)
    else:
        cache_k, cache_v, page_table = paged
        n_pool, page = cache_k.shape[:2]
        o_sb = _sb_sample_call(q.reshape(n_seq, seq_len, sbw), k32.reshape(n_seq, seq_len, sbw),
                               v32.reshape(n_seq, seq_len, sbw), cache_k.reshape(n_pool, page, sbw),
                               cache_v.reshape(n_pool, page, sbw), page_table, w.sb_bias,
                               dims.sb_heads, dims.sb_hd).reshape(t, sbw)

    kw = dims.gla_heads * dims.gla_dk
    gqk3 = gqk.reshape(n_seq, seq_len, 2 * kw)
    chunk = min(64, seq_len)
    o_gla, s_new = _gla_call(gqk3[:, :, :kw], gqk3[:, :, kw:], gv.reshape(n_seq, seq_len, -1),
                             gr.reshape(n_seq, seq_len, -1), gd.reshape(n_seq, seq_len, -1),
                             w.w_gk_up, w.b_gk, w.gla_onorm_g, gla_s0, chunk=chunk, valid=chunk,
                             rows_per_step=min(256, seq_len))
    o_gla = o_gla.reshape(t, -1)

    mixed = _merge_call(rows, h, o_sb, o_gla, w.w_msb, w.w_mgla, w.w_br_sb, w.w_br_gla)
    x1, h2 = _outproj_call(rows, mixed, x2, w.w_out, gt1, w.norm2_g, sc2, sh2)
    g, conv_new = _up_call(rows, h2, w.w_up, w.conv_w, w.conv_b, conv_state)
    y = _down_call(rows, g, w.w_down, x1, gt2)
    return (y.reshape(n_seq, seq_len, d), k32.reshape(n_seq, seq_len, dims.sb_heads, dims.sb_hd),
            v32.reshape(n_seq, seq_len, dims.sb_heads, dims.sb_hd), s_new, conv_new)


def kernel(x_prompt, x_sample, c_prompt, c_sample, cache_k_sb, cache_v_sb, page_table, state_gla, state_ffn_conv, norm1_g, norm2_g, w_mod, b_mod, w_in, q_norm_g, k_norm_g, sb_bias, w_gk_up, b_gk, gla_onorm_g, w_br_sb, w_br_gla, w_out, w_up, conv_w, conv_b, w_down):
    depth = w_in.shape[0]
    bp, lp, d = x_prompt.shape
    bs, ls, _ = x_sample.shape
    dims = _Dims(sb_heads=cache_k_sb.shape[3], sb_hd=cache_k_sb.shape[4], gla_heads=state_gla.shape[2],
                 gla_dk=state_gla.shape[3], gla_dv=state_gla.shape[4], rank=w_gk_up.shape[1])
    rows_p = _Rows(bp, lp, min(1024, lp))
    rows_s = _Rows(bs, ls, bs * ls)
    pad_rows = -(bp + bs) % 8
    c_all = jnp.concatenate([c_prompt, c_sample, jnp.zeros((pad_rows, d), F32)], axis=0)
    xp, xs = x_prompt, x_sample
    outs = [[] for _ in range(8)]
    for l in range(depth):
        w = _prep_weights(dims, norm1_g[l], norm2_g[l], w_in[l], q_norm_g[l], k_norm_g[l], sb_bias[l], w_gk_up[l],
                          b_gk[l], gla_onorm_g[l], w_br_sb[l], w_br_gla[l], w_out[l], w_up[l], conv_w[l],
                          conv_b[l], w_down[l])
        mod = _mod_call(c_all, w_mod[l], b_mod[l])
        zero_state = jnp.zeros((bp,) + state_gla.shape[2:], F32)
        xp, kp, vp, sp, cp = _layer_group(dims, w, rows_p, xp, mod[:bp], zero_state, None, None)
        xs, ks, vs, ss, cs = _layer_group(dims, w, rows_s, xs, mod[bp:bp + bs], state_gla[l], state_ffn_conv[l],
                                          (cache_k_sb[l], cache_v_sb[l], page_table))
        for lst, val in zip(outs, (kp, vp, sp, cp, ks, vs, ss, cs)):
            lst.append(val)
    return (xp, xs) + tuple(jnp.stack(lst) for lst in outs)
```

```python
import functools
import math
from typing import NamedTuple

import jax
import jax.numpy as jnp
import numpy as np
from jax import lax
from jax.experimental import pallas as pl
from jax.experimental.pallas import tpu as pltpu

F32 = jnp.float32
BF16 = jnp.bfloat16
EPS = 1e-6
GLA_TAU = 16.0
LOG2E = 1.4426950408889634
VMEM_LIMIT_BYTES = 48 * 1024 * 1024
LANES = 128
NT_DIMS = (((1,), (1,)), ((), ()))
TN_DIMS = (((0,), (0,)), ((), ()))


def _params(*sem):
    return pltpu.CompilerParams(dimension_semantics=sem, vmem_limit_bytes=VMEM_LIMIT_BYTES)


def _dot(a, b):
    return jnp.dot(a, b, preferred_element_type=F32)


def _log_sigmoid_neg(z):
    return -(jnp.maximum(z, 0.0) + jnp.log(1.0 + jnp.exp(-jnp.abs(z))))


def _softplus2(z):
    return jnp.maximum(z, 0.0) + jnp.log(1.0 + jnp.exp2(-jnp.abs(z))) * LOG2E


def _silu(x):
    return x * jax.nn.sigmoid(x)


class _Rows(NamedTuple):
    n_seq: int
    seq_len: int
    tm: int

    @property
    def total(self):
        return self.n_seq * self.seq_len

    @property
    def per_row(self):
        return self.seq_len < self.tm


def _seq_vec(rows, v, row_tile_of):
    d = v.shape[1]
    if rows.per_row:
        return jnp.repeat(v, rows.seq_len, axis=0), pl.BlockSpec((rows.tm, d), lambda *g: (row_tile_of(*g), 0))
    tiles_per_seq = rows.seq_len // rows.tm
    return v[:, None, :], pl.BlockSpec((None, 1, d), lambda *g: (row_tile_of(*g) // tiles_per_seq, 0, 0))


def _mod_kernel(c_ref, w_ref, b_ref, o_ref):
    c = c_ref[...]
    o_ref[...] = _dot(_silu(c).astype(BF16), w_ref[...].astype(BF16)) + b_ref[...]


def _mod_call(c, w_mod, b_mod):
    r, d = c.shape
    n = w_mod.shape[1]
    tn = 512
    return pl.pallas_call(
        _mod_kernel,
        grid=(n // tn,),
        in_specs=[pl.BlockSpec((r, d), lambda j: (0, 0)),
                  pl.BlockSpec((d, tn), lambda j: (0, j)),
                  pl.BlockSpec((1, tn), lambda j: (0, j))],
        out_specs=pl.BlockSpec((r, tn), lambda j: (0, j)),
        out_shape=jax.ShapeDtypeStruct((r, n), F32),
        compiler_params=_params("arbitrary"),
        name="mod",
    )(c, w_mod, b_mod.reshape(1, n))


def _rms_modulate(x, g, sc, sh):
    y = x * lax.rsqrt(jnp.mean(x * x, axis=-1, keepdims=True) + EPS) * g
    return y * (1.0 + sc) + sh


def _prenorm_kernel(x_ref, g_ref, sc_ref, sh_ref, o_ref):
    o_ref[...] = _rms_modulate(x_ref[...], g_ref[...], sc_ref[...], sh_ref[...]).astype(o_ref.dtype)


def _prenorm_call(rows, x, g, sc, sh):
    t, d = x.shape
    sc_a, sc_s = _seq_vec(rows, sc, lambda i: i)
    sh_a, sh_s = _seq_vec(rows, sh, lambda i: i)
    return pl.pallas_call(
        _prenorm_kernel,
        grid=(t // rows.tm,),
        in_specs=[pl.BlockSpec((rows.tm, d), lambda i: (i, 0)),
                  pl.BlockSpec((1, d), lambda i: (0, 0)), sc_s, sh_s],
        out_specs=pl.BlockSpec((rows.tm, d), lambda i: (i, 0)),
        out_shape=jax.ShapeDtypeStruct((t, d), BF16),
        compiler_params=_params("arbitrary"),
        name="prenorm",
    )(x, g.reshape(1, d), sc_a, sh_a)


def _proj_kernel(h_ref, w_ref, *rest, head_dim, use_col):
    idx = 0
    col_ref = p_ref = None
    if use_col:
        col_ref = rest[idx]
        idx += 1
    if head_dim:
        p_ref = rest[idx]
        idx += 1
    o_refs = rest[idx:]
    acc = _dot(h_ref[...], w_ref[...])
    if head_dim:
        pw = p_ref.shape[0]
        parts = []
        for s in range(acc.shape[1] // pw):
            a = acc[:, s * pw:(s + 1) * pw]
            ss = _dot((a * a).astype(BF16), p_ref[...])
            parts.append(a * lax.rsqrt(ss * (1.0 / head_dim) + EPS))
        acc = parts[0] if len(parts) == 1 else jnp.concatenate(parts, axis=1)
    if use_col:
        acc = acc * col_ref[...]
    for o_ref in o_refs:
        o_ref[...] = acc.astype(o_ref.dtype)


def _proj_call(rows, h, w, out_dtypes, col=None, head_dim=0, tn=512, name="proj"):
    t, d = h.shape
    n = w.shape[1]
    tn = min(tn, n)
    tm = rows.tm
    ins = [h, w]
    specs = [pl.BlockSpec((tm, d), lambda i, j: (i, 0)), pl.BlockSpec((d, tn), lambda i, j: (0, j))]
    if col is not None:
        ins.append(col.reshape(1, n).astype(F32))
        specs.append(pl.BlockSpec((1, tn), lambda i, j: (0, j)))
    if head_dim:
        pw = min(256, tn)
        blk = np.arange(pw) // head_dim
        ins.append(jnp.asarray(blk[:, None] == blk[None, :], BF16))
        specs.append(pl.BlockSpec((pw, pw), lambda i, j: (0, 0)))
    outs = pl.pallas_call(
        functools.partial(_proj_kernel, head_dim=head_dim, use_col=col is not None),
        grid=(t // tm, n // tn),
        in_specs=specs,
        out_specs=[pl.BlockSpec((tm, tn), lambda i, j: (i, j)) for _ in out_dtypes],
        out_shape=[jax.ShapeDtypeStruct((t, n), dt) for dt in out_dtypes],
        compiler_params=_params("arbitrary", "arbitrary"),
        name=name,
    )(*ins)
    return outs


def _sb_prompt_kernel(bias_ref, q_ref, k_ref, v_ref, u_ref, o_ref, *, tq, hd, n_pairs):
    group = pl.program_id(1)
    i = pl.program_id(2)
    pw = 2 * hd
    lane = lax.broadcasted_iota(jnp.int32, (tq, pw), 1)
    row = lax.broadcasted_iota(jnp.int32, (2 * tq, tq), 0)
    col = lax.broadcasted_iota(jnp.int32, (2 * tq, tq), 1)
    causal = col < jnp.where(row >= tq, row - tq, row)
    zero = jnp.zeros((tq, pw), q_ref.dtype)
    q2s, biases = [], []
    for p in range(n_pairs):
        q = q_ref[:, p * pw:(p + 1) * pw]
        q2s.append(jnp.concatenate([jnp.where(lane < hd, q, zero), jnp.where(lane >= hd, q, zero)], axis=0))
        head = 2 * (group * n_pairs + p)
        biases.append((bias_ref[head], bias_ref[head + 1]))

    def block(n, carry, masked):
        off = pl.multiple_of((i - n) * tq, tq)
        out = []
        for p, (acc, c) in enumerate(carry):
            lanes = slice(p * pw, (p + 1) * pw)
            z = lax.dot_general(q2s[p], k_ref[pl.ds(off, tq), lanes], NT_DIMS, preferred_element_type=F32)
            z = jnp.concatenate([z[:tq] + biases[p][0], z[tq:] + biases[p][1]], axis=0)
            sp = _softplus2(z)
            if masked:
                sp = jnp.where(causal, sp, 0.0)
            a = jnp.exp2(z + c + _dot(sp.astype(BF16), u_ref[...]))
            if masked:
                a = jnp.where(causal, a, 0.0)
            acc = acc + _dot(a.astype(BF16), v_ref[pl.ds(off, tq), lanes])
            out.append((acc, c - jnp.sum(sp, axis=-1, keepdims=True)))
        return tuple(out)

    init = tuple((jnp.zeros((2 * tq, pw), F32), jnp.zeros((2 * tq, 1), F32)) for _ in range(n_pairs))
    carry = lax.fori_loop(1, i + 1, lambda t, cr: block(t, cr, False), block(0, init, True))
    for p, (acc, _) in enumerate(carry):
        o_ref[:, p * pw:(p + 1) * pw] = jnp.where(lane < hd, acc[:tq], acc[tq:]).astype(o_ref.dtype)


def _sb_prompt_call(q, k, v, bias, n_seq, seq_len, hd, tq=256, n_pairs=4):
    t, width = q.shape
    gw = 2 * hd * n_pairs
    nq = seq_len // tq
    u = jnp.asarray(-np.tril(np.ones((tq, tq), np.float32)), BF16)
    return pl.pallas_call(
        functools.partial(_sb_prompt_kernel, tq=tq, hd=hd, n_pairs=n_pairs),
        grid=(n_seq, width // gw, nq),
        in_specs=[pl.BlockSpec(memory_space=pltpu.SMEM),
                  pl.BlockSpec((tq, gw), lambda b, g, i: (b * nq + i, g)),
                  pl.BlockSpec((seq_len, gw), lambda b, g, i: (b, g)),
                  pl.BlockSpec((seq_len, gw), lambda b, g, i: (b, g)),
                  pl.BlockSpec((tq, tq), lambda b, g, i: (0, 0))],
        out_specs=pl.BlockSpec((tq, gw), lambda b, g, i: (b * nq + i, g)),
        out_shape=jax.ShapeDtypeStruct((t, width), BF16),
        compiler_params=_params("arbitrary", "arbitrary", "arbitrary"),
        name="sb_prompt",
    )(bias, q, k, v, u)


def _sb_sample_kernel(pt_ref, qbd_ref, bias_ref, kn_ref, vn_ref, *rest, n_slots, n_heads, hd, page):
    del pt_ref
    kp_refs = rest[:n_slots]
    vp_refs = rest[n_slots:2 * n_slots]
    u_ref, o_ref, acc_ref, c_ref = rest[2 * n_slots:]
    s = pl.program_id(1)
    qbd = qbd_ref[...]
    n_rows, width = qbd.shape
    n_new = kn_ref.shape[0]

    def block(z, mask, weigh):
        n_keys = z.shape[1]
        z = z + bias_ref[...]
        sp = _softplus2(z)
        if mask is not None:
            sp = jnp.where(mask, sp, 0.0)
        incl = _dot(sp.astype(BF16), u_ref[:n_keys, :n_keys])
        c = c_ref[...]
        a = jnp.exp2(z + incl + jnp.concatenate([c] * (n_keys // c.shape[1]), axis=1))
        if mask is not None:
            a = jnp.where(mask, a, 0.0)
        acc_ref[...] += weigh(a.astype(BF16))
        c_ref[...] = c - jnp.sum(sp, axis=-1, keepdims=True)

    @pl.when(s == 0)
    def _():
        acc_ref[...] = jnp.zeros_like(acc_ref)
        c_ref[...] = jnp.zeros_like(c_ref)
        pad = jnp.zeros((page - n_new, width), F32)
        kb = jnp.concatenate([kn_ref[...], pad], axis=0).astype(BF16)
        vb = jnp.concatenate([vn_ref[...], pad], axis=0).astype(BF16)
        row = lax.broadcasted_iota(jnp.int32, (n_rows, page), 0)
        col = lax.broadcasted_iota(jnp.int32, (n_rows, page), 1)
        mask = col * n_heads < row - row % n_heads
        block(lax.dot_general(qbd, kb, NT_DIMS, preferred_element_type=F32), mask, lambda a: _dot(a, vb))

    @pl.when(s > 0)
    def _():
        kt = jnp.concatenate([r[...].astype(BF16) for r in kp_refs], axis=1)
        vt = jnp.concatenate([r[...].astype(BF16) for r in vp_refs], axis=1)
        block(_dot(qbd, kt), None, lambda a: lax.dot_general(a, vt, NT_DIMS, preferred_element_type=F32))

    @pl.when(s == pl.num_programs(1) - 1)
    def _():
        row = lax.broadcasted_iota(jnp.int32, (n_rows, width), 0)
        col = lax.broadcasted_iota(jnp.int32, (n_rows, width), 1)
        own = jnp.where(col // hd == row % n_heads, acc_ref[...], 0.0)
        o_ref[...] = jnp.sum(own.reshape(n_rows // n_heads, n_heads, width), axis=1).astype(o_ref.dtype)


def _sb_sample_call(q, k_new, v_new, cache_k, cache_v, page_table, bias, n_heads, hd, n_slots=4):
    bsz, n_q, width = q.shape
    page = cache_k.shape[2]
    n_pages = page_table.shape[1]
    n_steps = n_pages // n_slots
    n_rows = n_q * n_heads
    n_new = 8
    eye = jnp.eye(n_heads, dtype=q.dtype)
    qbd = jnp.einsum("bqhd,hg->bqhgd", q.reshape(bsz, n_q, n_heads, hd), eye).reshape(bsz, n_rows, width)
    bias_rows = jnp.tile(bias.astype(F32), n_q).reshape(n_rows, 1)
    pad = ((0, 0), (0, n_new - n_q), (0, 0))
    k_new, v_new = jnp.pad(k_new, pad), jnp.pad(v_new, pad)
    slot_of, pos_of = np.divmod(np.arange(n_slots * page), page)
    newer_or_same = (slot_of[:, None] < slot_of[None, :]) | (
        (slot_of[:, None] == slot_of[None, :]) & (pos_of[:, None] >= pos_of[None, :]))
    u = jnp.asarray(-newer_or_same.astype(np.float32), BF16)

    def page_spec(slot):
        def imap(b, s, pt):
            return pt[b, n_pages - 1 - (jnp.maximum(s, 1) - 1) * n_slots - slot], 0, 0
        return pl.BlockSpec((None, width, page), imap)

    per_seq = lambda b, s, pt: (b, 0, 0)
    const = lambda b, s, pt: (0, 0)
    grid_spec = pltpu.PrefetchScalarGridSpec(
        num_scalar_prefetch=1,
        grid=(bsz, n_steps + 1),
        in_specs=[pl.BlockSpec((None, n_rows, width), per_seq),
                  pl.BlockSpec((n_rows, 1), const),
                  pl.BlockSpec((None, n_new, width), per_seq),
                  pl.BlockSpec((None, n_new, width), per_seq)]
                 + [page_spec(slot) for slot in range(n_slots)] * 2
                 + [pl.BlockSpec(u.shape, const)],
        out_specs=pl.BlockSpec((None, n_q, width), per_seq),
        scratch_shapes=[pltpu.VMEM((n_rows, width), F32), pltpu.VMEM((n_rows, LANES), F32)],
    )
    return pl.pallas_call(
        functools.partial(_sb_sample_kernel, n_slots=n_slots, n_heads=n_heads, hd=hd, page=page),
        grid_spec=grid_spec,
        out_shape=jax.ShapeDtypeStruct((bsz, n_q, width), BF16),
        compiler_params=_params("arbitrary", "arbitrary"),
        name="sb_sample",
    )(page_table, qbd, bias_rows, k_new, v_new, *([cache_k] * n_slots), *([cache_v] * n_slots), u)


def _gla_tables(c):
    n_lev = int(math.log2(c))
    assert 1 << n_lev == c
    m = np.zeros((n_lev + 2, c, c), np.float32)
    level = np.full((c, c), -1, np.int32)
    idx = np.arange(c)
    for l in range(n_lev):
        half = c >> (l + 1)
        start = idx // (2 * half) * (2 * half)
        bound = start + half
        for t in range(c):
            if t >= bound[t]:
                m[l, t, bound[t]:t + 1] = 1.0
            else:
                m[l, t, t + 1:bound[t]] = 1.0
        same = start[:, None] == start[None, :]
        level[same & (idx[:, None] >= bound[:, None]) & (idx[None, :] < bound[None, :])] = l
    level[idx, idx] = n_lev
    for t in range(c):
        m[n_lev, t, :t + 1] = 1.0
        m[n_lev + 1, t, t + 1:] = 1.0
    return m.reshape((n_lev + 2) * c, c), level, n_lev


def _split3(x):
    hi = x.astype(BF16)
    r = x - hi.astype(F32)
    mid = r.astype(BF16)
    lo = (r - mid.astype(F32)).astype(BF16)
    return hi, mid, lo


def _gla_kernel(q_ref, k_ref, v_ref, gr_ref, gd_ref, wup_ref, bgk_ref, gno_ref, s0_ref, m_ref, lvl_ref,
                o_ref, sout_ref, s_ref, *, chunk, n_lev, rank, valid):
    rb = pl.program_id(2)

    @pl.when(rb == 0)
    def _():
        s_ref[...] = s0_ref[...]

    n_rows, dk = q_ref.shape
    level = lvl_ref[...]
    m = m_ref[...]
    ones = jnp.ones((chunk, dk), BF16)
    for ci in range(n_rows // chunk):
        sl = slice(ci * chunk, (ci + 1) * chunk)
        q, k, v = q_ref[sl, :], k_ref[sl, :], v_ref[sl, :]
        z = _dot(gd_ref[sl, :rank].astype(BF16), wup_ref[...]) + bgk_ref[...]
        g = _log_sigmoid_neg(-z) * (1.0 / GLA_TAU)
        if valid < chunk:
            g = jnp.where(lax.broadcasted_iota(jnp.int32, g.shape, 0) < valid, g, 0.0)
        g3 = _split3(g)
        x = jnp.exp(sum(_dot(m, gp) for gp in g3))
        scores = jnp.where(level == n_lev,
                           lax.dot_general(q.astype(BF16), k.astype(BF16), NT_DIMS, preferred_element_type=F32), 0.0)
        for l in range(n_lev):
            xl = x[l * chunk:(l + 1) * chunk]
            s_l = lax.dot_general((q * xl).astype(BF16), (k * xl).astype(BF16), NT_DIMS, preferred_element_type=F32)
            scores = jnp.where(level == l, s_l, scores)
        qb = (q * x[n_lev * chunk:(n_lev + 1) * chunk]).astype(BF16)
        kb = (k * x[(n_lev + 1) * chunk:]).astype(BF16)
        state = s_ref[...]
        o = _dot(scores.astype(BF16), v) + _dot(qb, state.astype(BF16))
        d_col = jnp.exp(sum(lax.dot_general(gp, ones, TN_DIMS, preferred_element_type=F32) for gp in g3))
        d_full = jnp.concatenate([d_col] * (state.shape[1] // dk), axis=1)
        s_ref[...] = state * d_full + lax.dot_general(kb, v, TN_DIMS, preferred_element_type=F32)
        on = o * lax.rsqrt(jnp.mean(o * o, axis=-1, keepdims=True) + EPS) * gno_ref[...]
        o_ref[sl, :] = (on * _silu(gr_ref[sl, :])).astype(o_ref.dtype)

    @pl.when(rb == pl.num_programs(2) - 1)
    def _():
        sout_ref[...] = s_ref[...]


def _gla_call(q, k, v, gr, gd, w_gk_up, b_gk, gno, s0, chunk, valid, rows_per_step):
    bsz, seq, kw = q.shape
    n_heads, dk, dv = s0.shape[1:]
    rank = w_gk_up.shape[0]
    m, level, n_lev = _gla_tables(chunk)
    r = rows_per_step
    blk = lambda w: pl.BlockSpec((None, r, w), lambda b, h, i: (b, i, h))
    const2 = lambda b, h, i: (0, 0)
    state_spec = pl.BlockSpec((None, None, dk, dv), lambda b, h, i: (b, h, 0, 0))
    return pl.pallas_call(
        functools.partial(_gla_kernel, chunk=chunk, n_lev=n_lev, rank=rank, valid=valid),
        grid=(bsz, n_heads, seq // r),
        in_specs=[blk(dk), blk(dk), blk(dv), blk(dv),
                  pl.BlockSpec((None, r, gd.shape[2]), lambda b, h, i: (b, i, 0)),
                  pl.BlockSpec((rank, dk), lambda b, h, i: (0, h)),
                  pl.BlockSpec((1, dk), lambda b, h, i: (0, h)),
                  pl.BlockSpec((1, dv), const2),
                  state_spec,
                  pl.BlockSpec(m.shape, const2),
                  pl.BlockSpec(level.shape, const2)],
        out_specs=[blk(dv), state_spec],
        out_shape=[jax.ShapeDtypeStruct((bsz, seq, n_heads * dv), BF16),
                   jax.ShapeDtypeStruct(s0.shape, F32)],
        scratch_shapes=[pltpu.VMEM((dk, dv), F32)],
        compiler_params=_params("arbitrary", "arbitrary", "arbitrary"),
        name="gla",
    )(q, k, v, gr, gd, w_gk_up.astype(BF16), b_gk.reshape(1, -1), gno.reshape(1, dv), s0,
      jnp.asarray(m, BF16), jnp.asarray(level))


def _merge_kernel(h_ref, osb_ref, ogla_ref, wmsb_ref, wmgla_ref, wbsb_ref, wbgla_ref, o_ref):
    h = h_ref[...]
    gate_sb = jax.nn.sigmoid(_dot(h, wmsb_ref[...]))
    gate_gla = jax.nn.sigmoid(_dot(h, wmgla_ref[...]))
    mixed = gate_sb * _dot(osb_ref[...], wbsb_ref[...]) + gate_gla * _dot(ogla_ref[...], wbgla_ref[...])
    o_ref[...] = mixed.astype(o_ref.dtype)


def _merge_call(rows, h, o_sb, o_gla, w_msb, w_mgla, w_br_sb, w_br_gla, tn=512):
    t, d = h.shape
    tm = min(rows.tm, 512)
    row = lambda w: pl.BlockSpec((tm, w), lambda i, j: (i, 0))
    wcol = lambda kdim: pl.BlockSpec((kdim, tn), lambda i, j: (0, j))
    return pl.pallas_call(
        _merge_kernel,
        grid=(t // tm, d // tn),
        in_specs=[row(d), row(o_sb.shape[1]), row(o_gla.shape[1]),
                  wcol(d), wcol(d), wcol(o_sb.shape[1]), wcol(o_gla.shape[1])],
        out_specs=pl.BlockSpec((tm, tn), lambda i, j: (i, j)),
        out_shape=jax.ShapeDtypeStruct((t, d), BF16),
        compiler_params=_params("arbitrary", "arbitrary"),
        name="merge",
    )(h, o_sb, o_gla, w_msb, w_mgla, w_br_sb, w_br_gla)


def _outproj_kernel(mixed_ref, x_ref, w_ref, gt_ref, g_ref, sc_ref, sh_ref, x1_ref, h2_ref):
    x1 = x_ref[...] + gt_ref[...] * _dot(mixed_ref[...], w_ref[...])
    x1_ref[...] = x1
    h2_ref[...] = _rms_modulate(x1, g_ref[...], sc_ref[...], sh_ref[...]).astype(h2_ref.dtype)


def _outproj_call(rows, mixed, x, w_out, gt, g2, sc, sh):
    t, d = x.shape
    rows = rows._replace(tm=min(rows.tm, 256))
    tm = rows.tm
    vecs = [_seq_vec(rows, v, lambda i: i) for v in (gt, sc, sh)]
    row = pl.BlockSpec((tm, d), lambda i: (i, 0))
    return pl.pallas_call(
        _outproj_kernel,
        grid=(t // tm,),
        in_specs=[row, row, pl.BlockSpec((d, d), lambda i: (0, 0)), vecs[0][1],
                  pl.BlockSpec((1, d), lambda i: (0, 0)), vecs[1][1], vecs[2][1]],
        out_specs=[row, row],
        out_shape=[jax.ShapeDtypeStruct((t, d), F32), jax.ShapeDtypeStruct((t, d), BF16)],
        compiler_params=_params("arbitrary"),
        name="outproj",
    )(mixed, x, w_out, vecs[0][0], g2.reshape(1, d), vecs[1][0], vecs[2][0])


CONV_TAIL = 8


def _conv_gate(a, a1, a2, u, cw_ref, cb_ref):
    conv = cb_ref[...] + cw_ref[0:1, :] * a2 + cw_ref[1:2, :] * a1 + cw_ref[2:3, :] * a
    return _silu(conv) * u


def _up_carry_kernel(h2_ref, wg_ref, wv_ref, cw_ref, cb_ref, g_ref, tail_ref, carry_ref, *, tiles_per_seq):
    i = pl.program_id(1)

    @pl.when(i % tiles_per_seq == 0)
    def _():
        carry_ref[...] = jnp.zeros_like(carry_ref)

    h2 = h2_ref[...]
    a = _dot(h2, wg_ref[...])
    u = _dot(h2, wv_ref[...])
    tm = a.shape[0]
    prev = carry_ref[...]
    head_row = lax.broadcasted_iota(jnp.int32, prev.shape, 0)
    shifted = []
    for shift in (1, 2):
        r = pltpu.roll(a, shift, 0)
        top = jnp.where(head_row < shift, pltpu.roll(prev, shift, 0), r[:CONV_TAIL])
        shifted.append(jnp.concatenate([top, r[CONV_TAIL:]], axis=0))
    tail = a[tm - CONV_TAIL:]
    carry_ref[...] = tail
    tail_ref[...] = tail
    g_ref[...] = _conv_gate(a, shifted[0], shifted[1], u, cw_ref, cb_ref).astype(g_ref.dtype)


def _up_state_kernel(h2_ref, wg_ref, wv_ref, cw_ref, cb_ref, e1_ref, e2_ref, g_ref, a_ref, *, seq_len):
    h2 = h2_ref[...]
    a = _dot(h2, wg_ref[...])
    u = _dot(h2, wv_ref[...])
    pos = lax.broadcasted_iota(jnp.int32, a.shape, 0) % seq_len
    a1 = jnp.where(pos < 1, e1_ref[...], pltpu.roll(a, 1, 0))
    a2 = jnp.where(pos < 2, e2_ref[...], pltpu.roll(a, 2, 0))
    a_ref[...] = a
    g_ref[...] = _conv_gate(a, a1, a2, u, cw_ref, cb_ref).astype(g_ref.dtype)


def _up_call(rows, h2, w_up, conv_w, conv_b, conv_state, tn=512):
    t, d = h2.shape
    f = w_up.shape[1] // 2
    n_col = f // tn
    tm = min(rows.tm, 512)
    common_in = [pl.BlockSpec((tm, d), lambda j, i: (i, 0)),
                 pl.BlockSpec((d, tn), lambda j, i: (0, j)),
                 pl.BlockSpec((d, tn), lambda j, i: (0, j + n_col)),
                 pl.BlockSpec(conv_w.shape[:1] + (tn,), lambda j, i: (0, j)),
                 pl.BlockSpec((1, tn), lambda j, i: (0, j))]
    tile = pl.BlockSpec((tm, tn), lambda j, i: (i, j))
    g_shape = jax.ShapeDtypeStruct((t, f), BF16)
    args = (h2, w_up, w_up, conv_w, conv_b.reshape(1, f))
    if rows.seq_len >= tm:
        tiles_per_seq = rows.seq_len // tm
        g, tail = pl.pallas_call(
            functools.partial(_up_carry_kernel, tiles_per_seq=tiles_per_seq),
            grid=(n_col, t // tm),
            in_specs=common_in,
            out_specs=[tile, pl.BlockSpec((None, CONV_TAIL, tn), lambda j, i: (i // tiles_per_seq, 0, j))],
            out_shape=[g_shape, jax.ShapeDtypeStruct((rows.n_seq, CONV_TAIL, f), F32)],
            scratch_shapes=[pltpu.VMEM((CONV_TAIL, tn), F32)],
            compiler_params=_params("arbitrary", "arbitrary"),
            name="up_carry",
        )(*args)
        return g, tail[:, CONV_TAIL - 2:, :]
    n_seq, seq_len = rows.n_seq, rows.seq_len
    zeros = jnp.zeros((n_seq, seq_len - 1, f), F32)
    e1 = jnp.concatenate([conv_state[:, 1:2], zeros], axis=1).reshape(t, f)
    e2 = jnp.concatenate([conv_state[:, 0:2], zeros[:, 1:]], axis=1).reshape(t, f)
    g, a = pl.pallas_call(
        functools.partial(_up_state_kernel, seq_len=seq_len),
        grid=(n_col, t // tm),
        in_specs=common_in + [tile, tile],
        out_specs=[tile, tile],
        out_shape=[g_shape, jax.ShapeDtypeStruct((t, f), F32)],
        compiler_params=_params("arbitrary", "arbitrary"),
        name="up_state",
    )(*args, e1, e2)
    return g, a.reshape(n_seq, seq_len, f)[:, seq_len - 2:, :]


def _down_kernel(g_ref, w_ref, x_ref, gt_ref, o_ref):
    o_ref[...] = x_ref[...] + gt_ref[...] * _dot(g_ref[...], w_ref[...])


def _down_call(rows, g, w_down, x1, gt, tn=512):
    t, f = g.shape
    d = w_down.shape[1]
    rows = rows._replace(tm=min(rows.tm, 512))
    tm = rows.tm
    gt_a, _ = _seq_vec(rows, gt, lambda i, j: i)
    if rows.per_row:
        gt_s = pl.BlockSpec((tm, tn), lambda i, j: (i, j))
    else:
        tiles_per_seq = rows.seq_len // tm
        gt_s = pl.BlockSpec((None, 1, tn), lambda i, j: (i // tiles_per_seq, 0, j))
    tile = pl.BlockSpec((tm, tn), lambda i, j: (i, j))
    return pl.pallas_call(
        _down_kernel,
        grid=(t // tm, d // tn),
        in_specs=[pl.BlockSpec((tm, f), lambda i, j: (i, 0)),
                  pl.BlockSpec((f, tn), lambda i, j: (0, j)), tile, gt_s],
        out_specs=tile,
        out_shape=jax.ShapeDtypeStruct((t, d), F32),
        compiler_params=_params("arbitrary", "arbitrary"),
        name="down",
    )(g, w_down, x1, gt_a)


class _Weights(NamedTuple):
    norm1_g: jax.Array
    norm2_g: jax.Array
    w_q: jax.Array
    w_k: jax.Array
    w_v: jax.Array
    w_gqk: jax.Array
    w_gv: jax.Array
    w_gr: jax.Array
    w_gd: jax.Array
    w_msb: jax.Array
    w_mgla: jax.Array
    q_col: jax.Array
    k_col: jax.Array
    gqk_col: jax.Array
    sb_bias: jax.Array
    w_gk_up: jax.Array
    b_gk: jax.Array
    gla_onorm_g: jax.Array
    w_br_sb: jax.Array
    w_br_gla: jax.Array
    w_out: jax.Array
    w_up: jax.Array
    conv_w: jax.Array
    conv_b: jax.Array
    w_down: jax.Array


class _Dims(NamedTuple):
    sb_heads: int
    sb_hd: int
    gla_heads: int
    gla_dk: int
    gla_dv: int
    rank: int


def _prep_weights(dims, norm1_g, norm2_g, w_in, q_norm_g, k_norm_g, sb_bias, w_gk_up, b_gk, gla_onorm_g,
                  w_br_sb, w_br_gla, w_out, w_up, conv_w, conv_b, w_down):
    d = w_in.shape[0]
    sbw = dims.sb_heads * dims.sb_hd
    kw = dims.gla_heads * dims.gla_dk
    vw = dims.gla_heads * dims.gla_dv
    sizes = (sbw, sbw, sbw, 2 * kw, vw, vw, dims.rank, d, d)
    offs = np.concatenate([[0], np.cumsum(sizes)])
    w_q, w_k, w_v, w_gqk, w_gv, w_gr, w_gd, w_msb, w_mgla = (
        w_in[:, offs[n]:offs[n + 1]].astype(BF16) for n in range(len(sizes)))
    w_gd = jnp.pad(w_gd, ((0, 0), (0, LANES - dims.rank)))
    return _Weights(
        norm1_g, norm2_g, w_q, w_k, w_v, w_gqk, w_gv, w_gr, w_gd, w_msb, w_mgla,
        q_col=jnp.tile(q_norm_g, dims.sb_heads) * (dims.sb_hd ** -0.5 * LOG2E),
        k_col=jnp.tile(k_norm_g, dims.sb_heads),
        gqk_col=jnp.concatenate([jnp.full((kw,), dims.gla_dk ** -0.5, F32), jnp.ones((kw,), F32)]),
        sb_bias=sb_bias * LOG2E, w_gk_up=w_gk_up, b_gk=b_gk, gla_onorm_g=gla_onorm_g,
        w_br_sb=w_br_sb.astype(BF16), w_br_gla=w_br_gla.astype(BF16), w_out=w_out.astype(BF16),
        w_up=w_up.astype(BF16), conv_w=conv_w, conv_b=conv_b, w_down=w_down.astype(BF16))


def _layer_group(dims, w, rows, x, mod, gla_s0, conv_state, paged):
    n_seq, seq_len, d = x.shape
    t = n_seq * seq_len
    x2 = x.reshape(t, d)
    sh1, sc1, gt1, sh2, sc2, gt2 = jnp.split(mod, 6, axis=-1)
    h = _prenorm_call(rows, x2, w.norm1_g, sc1, sh1)

    (q,) = _proj_call(rows, h, w.w_q, (BF16,), col=w.q_col, head_dim=dims.sb_hd, name="proj_q")
    k32, k16 = _proj_call(rows, h, w.w_k, (F32, BF16), col=w.k_col, head_dim=dims.sb_hd, name="proj_k")
    v32, v16 = _proj_call(rows, h, w.w_v, (F32, BF16), name="proj_v")
    (gqk,) = _proj_call(rows, h, w.w_gqk, (F32,), col=w.gqk_col, name="proj_gqk")
    (gv,) = _proj_call(rows, h, w.w_gv, (BF16,), name="proj_gv")
    (gr,) = _proj_call(rows, h, w.w_gr, (F32,), name="proj_gr")
    (gd,) = _proj_call(rows, h, w.w_gd, (F32,), name="proj_gd")

    sbw = dims.sb_heads * dims.sb_hd
    if paged is None:
        o_sb = _sb_prompt_call(q, k16, v16, w.sb_bias, n_seq, seq_len, dims.sb_hd)
    else:
        cache_k, cache_v, page_table = paged
        n_pool, page = cache_k.shape[:2]
        channel_major = lambda c: jnp.transpose(c, (0, 2, 3, 1)).reshape(n_pool, sbw, page)
        o_sb = _sb_sample_call(q.reshape(n_seq, seq_len, sbw), k32.reshape(n_seq, seq_len, sbw),
                               v32.reshape(n_seq, seq_len, sbw), channel_major(cache_k), channel_major(cache_v),
                               page_table, w.sb_bias, dims.sb_heads, dims.sb_hd).reshape(t, sbw)

    kw = dims.gla_heads * dims.gla_dk
    gqk3 = gqk.reshape(n_seq, seq_len, 2 * kw)
    chunk = min(64, seq_len)
    o_gla, s_new = _gla_call(gqk3[:, :, :kw], gqk3[:, :, kw:], gv.reshape(n_seq, seq_len, -1),
                             gr.reshape(n_seq, seq_len, -1), gd.reshape(n_seq, seq_len, -1),
                             w.w_gk_up, w.b_gk, w.gla_onorm_g, gla_s0, chunk=chunk, valid=chunk,
                             rows_per_step=min(256, seq_len))
    o_gla = o_gla.reshape(t, -1)

    mixed = _merge_call(rows, h, o_sb, o_gla, w.w_msb, w.w_mgla, w.w_br_sb, w.w_br_gla)
    x1, h2 = _outproj_call(rows, mixed, x2, w.w_out, gt1, w.norm2_g, sc2, sh2)
    g, conv_new = _up_call(rows, h2, w.w_up, w.conv_w, w.conv_b, conv_state)
    y = _down_call(rows, g, w.w_down, x1, gt2)
    return (y.reshape(n_seq, seq_len, d), k32.reshape(n_seq, seq_len, dims.sb_heads, dims.sb_hd),
            v32.reshape(n_seq, seq_len, dims.sb_heads, dims.sb_hd), s_new, conv_new)


def kernel(x_prompt, x_sample, c_prompt, c_sample, cache_k_sb, cache_v_sb, page_table, state_gla, state_ffn_conv, norm1_g, norm2_g, w_mod, b_mod, w_in, q_norm_g, k_norm_g, sb_bias, w_gk_up, b_gk, gla_onorm_g, w_br_sb, w_br_gla, w_out, w_up, conv_w, conv_b, w_down):
    depth = w_in.shape[0]
    bp, lp, d = x_prompt.shape
    bs, ls, _ = x_sample.shape
    dims = _Dims(sb_heads=cache_k_sb.shape[3], sb_hd=cache_k_sb.shape[4], gla_heads=state_gla.shape[2],
                 gla_dk=state_gla.shape[3], gla_dv=state_gla.shape[4], rank=w_gk_up.shape[1])
    rows_p = _Rows(bp, lp, min(1024, lp))
    rows_s = _Rows(bs, ls, bs * ls)
    pad_rows = -(bp + bs) % 8
    c_all = jnp.concatenate([c_prompt, c_sample, jnp.zeros((pad_rows, d), F32)], axis=0)
    xp, xs = x_prompt, x_sample
    outs = [[] for _ in range(8)]
    for l in range(depth):
        w = _prep_weights(dims, norm1_g[l], norm2_g[l], w_in[l], q_norm_g[l], k_norm_g[l], sb_bias[l], w_gk_up[l],
                          b_gk[l], gla_onorm_g[l], w_br_sb[l], w_br_gla[l], w_out[l], w_up[l], conv_w[l],
                          conv_b[l], w_down[l])
        mod = _mod_call(c_all, w_mod[l], b_mod[l])
        zero_state = jnp.zeros((bp,) + state_gla.shape[2:], F32)
        xp, kp, vp, sp, cp = _layer_group(dims, w, rows_p, xp, mod[:bp], zero_state, None, None)
        xs, ks, vs, ss, cs = _layer_group(dims, w, rows_s, xs, mod[bp:bp + bs], state_gla[l], state_ffn_conv[l],
                                          (cache_k_sb[l], cache_v_sb[l], page_table))
        for lst, val in zip(outs, (kp, vp, sp, cp, ks, vs, ss, cs)):
            lst.append(val)
    return (xp, xs) + tuple(jnp.stack(lst) for lst in outs)
```

```python
import functools
import math
from typing import NamedTuple

import jax
import jax.numpy as jnp
import numpy as np
from jax import lax
from jax.experimental import pallas as pl
from jax.experimental.pallas import tpu as pltpu

F32 = jnp.float32
BF16 = jnp.bfloat16
EPS = 1e-6
GLA_TAU = 16.0
LOG2E = 1.4426950408889634
VMEM_LIMIT_BYTES = 48 * 1024 * 1024
LANES = 128
NT_DIMS = (((1,), (1,)), ((), ()))
TN_DIMS = (((0,), (0,)), ((), ()))


def _params(*sem):
    return pltpu.CompilerParams(dimension_semantics=sem, vmem_limit_bytes=VMEM_LIMIT_BYTES)


def _dot(a, b):
    return jnp.dot(a, b, preferred_element_type=F32)


def _log_sigmoid_neg(z):
    return -(jnp.maximum(z, 0.0) + jnp.log(1.0 + jnp.exp(-jnp.abs(z))))


def _softplus2(z):
    return jnp.maximum(z, 0.0) + jnp.log(1.0 + jnp.exp2(-jnp.abs(z))) * LOG2E


def _silu(x):
    return x * jax.nn.sigmoid(x)


class _Rows(NamedTuple):
    n_seq: int
    seq_len: int
    tm: int

    @property
    def total(self):
        return self.n_seq * self.seq_len

    @property
    def per_row(self):
        return self.seq_len < self.tm


def _seq_vec(rows, v, row_tile_of):
    d = v.shape[1]
    if rows.per_row:
        return jnp.repeat(v, rows.seq_len, axis=0), pl.BlockSpec((rows.tm, d), lambda *g: (row_tile_of(*g), 0))
    tiles_per_seq = rows.seq_len // rows.tm
    return v[:, None, :], pl.BlockSpec((None, 1, d), lambda *g: (row_tile_of(*g) // tiles_per_seq, 0, 0))


def _mod_kernel(c_ref, w_ref, b_ref, o_ref):
    c = c_ref[...]
    o_ref[...] = _dot(_silu(c).astype(BF16), w_ref[...].astype(BF16)) + b_ref[...]


def _mod_call(c, w_mod, b_mod):
    r, d = c.shape
    n = w_mod.shape[1]
    tn = 512
    return pl.pallas_call(
        _mod_kernel,
        grid=(n // tn,),
        in_specs=[pl.BlockSpec((r, d), lambda j: (0, 0)),
                  pl.BlockSpec((d, tn), lambda j: (0, j)),
                  pl.BlockSpec((1, tn), lambda j: (0, j))],
        out_specs=pl.BlockSpec((r, tn), lambda j: (0, j)),
        out_shape=jax.ShapeDtypeStruct((r, n), F32),
        compiler_params=_params("arbitrary"),
        name="mod",
    )(c, w_mod, b_mod.reshape(1, n))


def _rms_modulate(x, g, sc, sh):
    y = x * lax.rsqrt(jnp.mean(x * x, axis=-1, keepdims=True) + EPS) * g
    return y * (1.0 + sc) + sh


def _prenorm_kernel(x_ref, g_ref, sc_ref, sh_ref, o_ref):
    o_ref[...] = _rms_modulate(x_ref[...], g_ref[...], sc_ref[...], sh_ref[...]).astype(o_ref.dtype)


def _prenorm_call(rows, x, g, sc, sh):
    t, d = x.shape
    sc_a, sc_s = _seq_vec(rows, sc, lambda i: i)
    sh_a, sh_s = _seq_vec(rows, sh, lambda i: i)
    return pl.pallas_call(
        _prenorm_kernel,
        grid=(t // rows.tm,),
        in_specs=[pl.BlockSpec((rows.tm, d), lambda i: (i, 0)),
                  pl.BlockSpec((1, d), lambda i: (0, 0)), sc_s, sh_s],
        out_specs=pl.BlockSpec((rows.tm, d), lambda i: (i, 0)),
        out_shape=jax.ShapeDtypeStruct((t, d), BF16),
        compiler_params=_params("arbitrary"),
        name="prenorm",
    )(x, g.reshape(1, d), sc_a, sh_a)


def _proj_kernel(h_ref, w_ref, *rest, head_dim, use_col):
    idx = 0
    col_ref = p_ref = None
    if use_col:
        col_ref = rest[idx]
        idx += 1
    if head_dim:
        p_ref = rest[idx]
        idx += 1
    o_refs = rest[idx:]
    acc = _dot(h_ref[...], w_ref[...])
    if head_dim:
        pw = p_ref.shape[0]
        parts = []
        for s in range(acc.shape[1] // pw):
            a = acc[:, s * pw:(s + 1) * pw]
            ss = _dot((a * a).astype(BF16), p_ref[...])
            parts.append(a * lax.rsqrt(ss * (1.0 / head_dim) + EPS))
        acc = parts[0] if len(parts) == 1 else jnp.concatenate(parts, axis=1)
    if use_col:
        acc = acc * col_ref[...]
    for o_ref in o_refs:
        o_ref[...] = acc.astype(o_ref.dtype)


def _proj_call(rows, h, w, out_dtypes, col=None, head_dim=0, tn=512, name="proj"):
    t, d = h.shape
    n = w.shape[1]
    tn = min(tn, n)
    tm = rows.tm
    ins = [h, w]
    specs = [pl.BlockSpec((tm, d), lambda i, j: (i, 0)), pl.BlockSpec((d, tn), lambda i, j: (0, j))]
    if col is not None:
        ins.append(col.reshape(1, n).astype(F32))
        specs.append(pl.BlockSpec((1, tn), lambda i, j: (0, j)))
    if head_dim:
        pw = min(256, tn)
        blk = np.arange(pw) // head_dim
        ins.append(jnp.asarray(blk[:, None] == blk[None, :], BF16))
        specs.append(pl.BlockSpec((pw, pw), lambda i, j: (0, 0)))
    outs = pl.pallas_call(
        functools.partial(_proj_kernel, head_dim=head_dim, use_col=col is not None),
        grid=(t // tm, n // tn),
        in_specs=specs,
        out_specs=[pl.BlockSpec((tm, tn), lambda i, j: (i, j)) for _ in out_dtypes],
        out_shape=[jax.ShapeDtypeStruct((t, n), dt) for dt in out_dtypes],
        compiler_params=_params("arbitrary", "arbitrary"),
        name=name,
    )(*ins)
    return outs


def _sb_prompt_kernel(bias_ref, q_ref, k_ref, v_ref, u_ref, o_ref, *, tq, hd, n_pairs):
    group = pl.program_id(1)
    i = pl.program_id(2)
    pw = 2 * hd
    lane = lax.broadcasted_iota(jnp.int32, (tq, pw), 1)
    row = lax.broadcasted_iota(jnp.int32, (2 * tq, tq), 0)
    col = lax.broadcasted_iota(jnp.int32, (2 * tq, tq), 1)
    causal = col < jnp.where(row >= tq, row - tq, row)
    zero = jnp.zeros((tq, pw), q_ref.dtype)
    q2s, biases = [], []
    for p in range(n_pairs):
        q = q_ref[:, p * pw:(p + 1) * pw]
        q2s.append(jnp.concatenate([jnp.where(lane < hd, q, zero), jnp.where(lane >= hd, q, zero)], axis=0))
        head = 2 * (group * n_pairs + p)
        biases.append((bias_ref[head], bias_ref[head + 1]))

    def block(n, carry, masked):
        off = pl.multiple_of((i - n) * tq, tq)
        out = []
        for p, (acc, c) in enumerate(carry):
            lanes = slice(p * pw, (p + 1) * pw)
            z = lax.dot_general(q2s[p], k_ref[pl.ds(off, tq), lanes], NT_DIMS, preferred_element_type=F32)
            z = jnp.concatenate([z[:tq] + biases[p][0], z[tq:] + biases[p][1]], axis=0)
            sp = _softplus2(z)
            if masked:
                sp = jnp.where(causal, sp, 0.0)
            a = jnp.exp2(z + c + _dot(sp.astype(BF16), u_ref[...]))
            if masked:
                a = jnp.where(causal, a, 0.0)
            acc = acc + _dot(a.astype(BF16), v_ref[pl.ds(off, tq), lanes])
            out.append((acc, c - jnp.sum(sp, axis=-1, keepdims=True)))
        return tuple(out)

    init = tuple((jnp.zeros((2 * tq, pw), F32), jnp.zeros((2 * tq, 1), F32)) for _ in range(n_pairs))
    carry = lax.fori_loop(1, i + 1, lambda t, cr: block(t, cr, False), block(0, init, True))
    for p, (acc, _) in enumerate(carry):
        o_ref[:, p * pw:(p + 1) * pw] = jnp.where(lane < hd, acc[:tq], acc[tq:]).astype(o_ref.dtype)


def _sb_prompt_call(q, k, v, bias, n_seq, seq_len, hd, tq=256, n_pairs=4):
    t, width = q.shape
    gw = 2 * hd * n_pairs
    nq = seq_len // tq
    u = jnp.asarray(-np.tril(np.ones((tq, tq), np.float32)), BF16)
    return pl.pallas_call(
        functools.partial(_sb_prompt_kernel, tq=tq, hd=hd, n_pairs=n_pairs),
        grid=(n_seq, width // gw, nq),
        in_specs=[pl.BlockSpec(memory_space=pltpu.SMEM),
                  pl.BlockSpec((tq, gw), lambda b, g, i: (b * nq + i, g)),
                  pl.BlockSpec((seq_len, gw), lambda b, g, i: (b, g)),
                  pl.BlockSpec((seq_len, gw), lambda b, g, i: (b, g)),
                  pl.BlockSpec((tq, tq), lambda b, g, i: (0, 0))],
        out_specs=pl.BlockSpec((tq, gw), lambda b, g, i: (b * nq + i, g)),
        out_shape=jax.ShapeDtypeStruct((t, width), BF16),
        compiler_params=_params("arbitrary", "arbitrary", "arbitrary"),
        name="sb_prompt",
    )(bias, q, k, v, u)


def _sb_sample_kernel(pt_ref, qbd_ref, bias_ref, kn_ref, vn_ref, *rest, n_slots, n_heads, hd, page):
    del pt_ref
    kp_refs = rest[:n_slots]
    vp_refs = rest[n_slots:2 * n_slots]
    u_ref, o_ref, acc_ref, c_ref = rest[2 * n_slots:]
    s = pl.program_id(1)
    qbd = qbd_ref[...]
    n_rows, width = qbd.shape
    n_new = kn_ref.shape[0]

    def block(z, mask, weigh):
        n_keys = z.shape[1]
        gsz = min(n_keys, u_ref.shape[0])
        z = z + bias_ref[...]
        sp = _softplus2(z)
        if mask is not None:
            sp = jnp.where(mask, sp, 0.0)
        c = c_ref[...]
        parts = []
        for g0 in range(0, n_keys, gsz):
            spg = sp[:, g0:g0 + gsz]
            incl = _dot(spg.astype(BF16), u_ref[:gsz, :gsz])
            parts.append(jnp.exp2(z[:, g0:g0 + gsz] + incl + jnp.concatenate([c] * (gsz // c.shape[1]), axis=1)))
            c = c - jnp.sum(spg, axis=-1, keepdims=True)
        a = parts[0] if len(parts) == 1 else jnp.concatenate(parts, axis=1)
        if mask is not None:
            a = jnp.where(mask, a, 0.0)
        acc_ref[...] += weigh(a.astype(BF16))
        c_ref[...] = c

    @pl.when(s == 0)
    def _():
        acc_ref[...] = jnp.zeros_like(acc_ref)
        c_ref[...] = jnp.zeros_like(c_ref)
        pad = jnp.zeros((page - n_new, width), F32)
        kb = jnp.concatenate([kn_ref[...], pad], axis=0).astype(BF16)
        vb = jnp.concatenate([vn_ref[...], pad], axis=0).astype(BF16)
        row = lax.broadcasted_iota(jnp.int32, (n_rows, page), 0)
        col = lax.broadcasted_iota(jnp.int32, (n_rows, page), 1)
        mask = col * n_heads < row - row % n_heads
        block(lax.dot_general(qbd, kb, NT_DIMS, preferred_element_type=F32), mask, lambda a: _dot(a, vb))

    @pl.when(s > 0)
    def _():
        kt = jnp.concatenate([r[...].astype(BF16) for r in kp_refs], axis=1)
        vt = jnp.concatenate([r[...].astype(BF16) for r in vp_refs], axis=1)
        block(_dot(qbd, kt), None, lambda a: lax.dot_general(a, vt, NT_DIMS, preferred_element_type=F32))

    @pl.when(s == pl.num_programs(1) - 1)
    def _():
        row = lax.broadcasted_iota(jnp.int32, (n_rows, width), 0)
        col = lax.broadcasted_iota(jnp.int32, (n_rows, width), 1)
        own = jnp.where(col // hd == row % n_heads, acc_ref[...], 0.0)
        o_ref[...] = jnp.sum(own.reshape(n_rows // n_heads, n_heads, width), axis=1).astype(o_ref.dtype)


def _sb_sample_call(q, k_new, v_new, cache_k, cache_v, page_table, bias, n_heads, hd, n_slots=8):
    bsz, n_q, width = q.shape
    page = cache_k.shape[2]
    n_pages = page_table.shape[1]
    n_steps = n_pages // n_slots
    n_rows = n_q * n_heads
    n_new = 8
    eye = jnp.eye(n_heads, dtype=q.dtype)
    qbd = jnp.einsum("bqhd,hg->bqhgd", q.reshape(bsz, n_q, n_heads, hd), eye).reshape(bsz, n_rows, width)
    bias_rows = jnp.tile(bias.astype(F32), n_q).reshape(n_rows, 1)
    pad = ((0, 0), (0, n_new - n_q), (0, 0))
    k_new, v_new = jnp.pad(k_new, pad), jnp.pad(v_new, pad)
    slot_of, pos_of = np.divmod(np.arange(min(n_slots, 4) * page), page)
    newer_or_same = (slot_of[:, None] < slot_of[None, :]) | (
        (slot_of[:, None] == slot_of[None, :]) & (pos_of[:, None] >= pos_of[None, :]))
    u = jnp.asarray(-newer_or_same.astype(np.float32), BF16)

    def page_spec(slot):
        def imap(b, s, pt):
            return pt[b, n_pages - 1 - (jnp.maximum(s, 1) - 1) * n_slots - slot], 0, 0
        return pl.BlockSpec((None, width, page), imap)

    per_seq = lambda b, s, pt: (b, 0, 0)
    const = lambda b, s, pt: (0, 0)
    grid_spec = pltpu.PrefetchScalarGridSpec(
        num_scalar_prefetch=1,
        grid=(bsz, n_steps + 1),
        in_specs=[pl.BlockSpec((None, n_rows, width), per_seq),
                  pl.BlockSpec((n_rows, 1), const),
                  pl.BlockSpec((None, n_new, width), per_seq),
                  pl.BlockSpec((None, n_new, width), per_seq)]
                 + [page_spec(slot) for slot in range(n_slots)] * 2
                 + [pl.BlockSpec(u.shape, const)],
        out_specs=pl.BlockSpec((None, n_q, width), per_seq),
        scratch_shapes=[pltpu.VMEM((n_rows, width), F32), pltpu.VMEM((n_rows, LANES), F32)],
    )
    return pl.pallas_call(
        functools.partial(_sb_sample_kernel, n_slots=n_slots, n_heads=n_heads, hd=hd, page=page),
        grid_spec=grid_spec,
        out_shape=jax.ShapeDtypeStruct((bsz, n_q, width), BF16),
        compiler_params=_params("arbitrary", "arbitrary"),
        name="sb_sample",
    )(page_table, qbd, bias_rows, k_new, v_new, *([cache_k] * n_slots), *([cache_v] * n_slots), u)


def _gla_tables(c):
    n_lev = int(math.log2(c))
    assert 1 << n_lev == c
    level = np.full((c, c), -1, np.int32)
    sign = np.zeros((n_lev, c, LANES), np.float32)
    idx = np.arange(c)
    for l in range(n_lev):
        half = c >> (l + 1)
        start = idx // (2 * half) * (2 * half)
        bound = start + half
        same = start[:, None] == start[None, :]
        level[same & (idx[:, None] >= bound[:, None]) & (idx[None, :] < bound[None, :])] = l
        sign[l] = np.where(idx >= bound, 1.0, -1.0)[:, None]
    level[idx, idx] = n_lev
    return np.tril(np.ones((c, c), np.float32)), level, sign.reshape(n_lev * c, LANES), n_lev


def _boundary_rows(b, half):
    c, w = b.shape
    sublanes = 8
    if 2 * half >= sublanes:
        return jnp.concatenate([jnp.broadcast_to(b[s + half - 1:s + half], (2 * half, w))
                                for s in range(0, c, 2 * half)], axis=0)
    b3 = b.reshape(c // sublanes, sublanes, w)
    sub = lax.broadcasted_iota(jnp.int32, b3.shape, 1)
    out = None
    for s in range(0, sublanes, 2 * half):
        piece = jnp.broadcast_to(b3[:, s + half - 1:s + half, :], b3.shape)
        out = piece if out is None else jnp.where(sub >= s, piece, out)
    return out.reshape(c, w)


def _pad_rows(x, n):
    if x.shape[0] == n:
        return x
    return jnp.concatenate([x, jnp.zeros((n - x.shape[0],) + x.shape[1:], x.dtype)], axis=0)


def _split3(x):
    hi = x.astype(BF16)
    r = x - hi.astype(F32)
    mid = r.astype(BF16)
    lo = (r - mid.astype(F32)).astype(BF16)
    return hi, mid, lo


def _gla_kernel(q_ref, k_ref, v_ref, gr_ref, gd_ref, wup_ref, bgk_ref, gno_ref, s0_ref, tri_ref, lvl_ref, sgn_ref,
                o_ref, sout_ref, s_ref, *, chunk, n_lev, rank):
    rb = pl.program_id(1)

    @pl.when(rb == 0)
    def _():
        s_ref[...] = s0_ref[...]

    n_rows = q_ref.shape[0]
    n_heads, dk, dv = s_ref.shape
    level = lvl_ref[...]
    tri = tri_ref[...]
    ones = jnp.ones((chunk, dk), BF16)
    for start in range(0, n_rows, chunk):
        valid = min(chunk, n_rows - start)
        rows = slice(start, start + valid)
        z = _dot(_pad_rows(gd_ref[rows, :rank], chunk).astype(BF16), wup_ref[...]) + bgk_ref[...]
        g = _log_sigmoid_neg(-z) * (1.0 / GLA_TAU)
        if valid < chunk:
            g = jnp.where(lax.broadcasted_iota(jnp.int32, g.shape, 0) < valid, g, 0.0)
        g3 = _split3(g)
        b = sum(_dot(tri, gp) for gp in g3)
        q = _pad_rows(q_ref[rows, :], chunk)
        k = _pad_rows(k_ref[rows, :], chunk)
        factored = [(q.astype(BF16), k.astype(BF16))]
        for l in range(n_lev):
            sgn = jnp.concatenate([sgn_ref[l * chunk:(l + 1) * chunk, :]] * (q.shape[1] // LANES), axis=1)
            x = jnp.exp((b - _boundary_rows(b, chunk >> (l + 1))) * sgn)
            factored.append(((q * x).astype(BF16), (k * x).astype(BF16)))
        qb = (q * jnp.exp(b)).astype(BF16)
        kb = (k * jnp.exp(b[chunk - 1:chunk] - b)).astype(BF16)
        for h in range(n_heads):
            kl = slice(h * dk, (h + 1) * dk)
            vl = slice(h * dv, (h + 1) * dv)
            scores = jnp.zeros((chunk, chunk), F32)
            for l, (ql, kf) in enumerate(factored):
                s_l = lax.dot_general(ql[:, kl], kf[:, kl], NT_DIMS, preferred_element_type=F32)
                scores = jnp.where(level == (l - 1 if l else n_lev), s_l, scores)
            v = _pad_rows(v_ref[rows, vl].astype(F32), chunk).astype(BF16)
            state = s_ref[h]
            o = _dot(scores.astype(BF16), v) + _dot(qb[:, kl], state.astype(BF16))
            d_col = jnp.exp(sum(lax.dot_general(gp[:, kl], ones, TN_DIMS, preferred_element_type=F32) for gp in g3))
            s_ref[h] = (state * jnp.concatenate([d_col] * (dv // dk), axis=1)
                        + lax.dot_general(kb[:, kl], v, TN_DIMS, preferred_element_type=F32))
            on = o * lax.rsqrt(jnp.mean(o * o, axis=-1, keepdims=True) + EPS) * gno_ref[...]
            o_ref[rows, vl] = (on[:valid] * _silu(gr_ref[rows, vl])).astype(o_ref.dtype)

    @pl.when(rb == pl.num_programs(1) - 1)
    def _():
        sout_ref[...] = s_ref[...]


def _gla_call(qk, v, gr, gd, w_gk_up, b_gk, gno, s0, chunk, rows_per_step):
    bsz, seq = qk.shape[:2]
    kw = qk.shape[2] // 2
    n_heads, dk, dv = s0.shape[1:]
    rank = w_gk_up.shape[0]
    tri, level, sign, n_lev = _gla_tables(chunk)
    r = rows_per_step
    blk = lambda w: pl.BlockSpec((None, r, w), lambda b, i: (b, i, 0))
    const2 = lambda b, i: (0, 0)
    full = lambda a: pl.BlockSpec(a.shape, const2)
    state_spec = pl.BlockSpec((None, n_heads, dk, dv), lambda b, i: (b, 0, 0, 0))
    return pl.pallas_call(
        functools.partial(_gla_kernel, chunk=chunk, n_lev=n_lev, rank=rank),
        grid=(bsz, seq // r),
        in_specs=[blk(kw), pl.BlockSpec((None, r, kw), lambda b, i: (b, i, 1)),
                  blk(n_heads * dv), blk(n_heads * dv), blk(gd.shape[2]),
                  pl.BlockSpec((rank, kw), const2), pl.BlockSpec((1, kw), const2), pl.BlockSpec((1, dv), const2),
                  state_spec, full(tri), full(level), full(sign)],
        out_specs=[blk(n_heads * dv), state_spec],
        out_shape=[jax.ShapeDtypeStruct((bsz, seq, n_heads * dv), BF16),
                   jax.ShapeDtypeStruct(s0.shape, F32)],
        scratch_shapes=[pltpu.VMEM((n_heads, dk, dv), F32)],
        compiler_params=_params("arbitrary", "arbitrary"),
        name="gla",
    )(qk, qk, v, gr, gd, w_gk_up.astype(BF16), b_gk.reshape(1, -1), gno.reshape(1, dv), s0,
      jnp.asarray(tri, BF16), jnp.asarray(level), jnp.asarray(sign))


def _merge_kernel(h_ref, osb_ref, ogla_ref, wmsb_ref, wmgla_ref, wbsb_ref, wbgla_ref, o_ref):
    h = h_ref[...]
    gate_sb = jax.nn.sigmoid(_dot(h, wmsb_ref[...]))
    gate_gla = jax.nn.sigmoid(_dot(h, wmgla_ref[...]))
    mixed = gate_sb * _dot(osb_ref[...], wbsb_ref[...]) + gate_gla * _dot(ogla_ref[...], wbgla_ref[...])
    o_ref[...] = mixed.astype(o_ref.dtype)


def _merge_call(rows, h, o_sb, o_gla, w_msb, w_mgla, w_br_sb, w_br_gla, tn=512):
    t, d = h.shape
    tm = min(rows.tm, 512)
    row = lambda w: pl.BlockSpec((tm, w), lambda i, j: (i, 0))
    wcol = lambda kdim: pl.BlockSpec((kdim, tn), lambda i, j: (0, j))
    return pl.pallas_call(
        _merge_kernel,
        grid=(t // tm, d // tn),
        in_specs=[row(d), row(o_sb.shape[1]), row(o_gla.shape[1]),
                  wcol(d), wcol(d), wcol(o_sb.shape[1]), wcol(o_gla.shape[1])],
        out_specs=pl.BlockSpec((tm, tn), lambda i, j: (i, j)),
        out_shape=jax.ShapeDtypeStruct((t, d), BF16),
        compiler_params=_params("arbitrary", "arbitrary"),
        name="merge",
    )(h, o_sb, o_gla, w_msb, w_mgla, w_br_sb, w_br_gla)


def _outproj_kernel(mixed_ref, x_ref, w_ref, gt_ref, g_ref, sc_ref, sh_ref, x1_ref, h2_ref):
    x1 = x_ref[...] + gt_ref[...] * _dot(mixed_ref[...], w_ref[...])
    x1_ref[...] = x1
    h2_ref[...] = _rms_modulate(x1, g_ref[...], sc_ref[...], sh_ref[...]).astype(h2_ref.dtype)


def _outproj_call(rows, mixed, x, w_out, gt, g2, sc, sh):
    t, d = x.shape
    rows = rows._replace(tm=min(rows.tm, 256))
    tm = rows.tm
    vecs = [_seq_vec(rows, v, lambda i: i) for v in (gt, sc, sh)]
    row = pl.BlockSpec((tm, d), lambda i: (i, 0))
    return pl.pallas_call(
        _outproj_kernel,
        grid=(t // tm,),
        in_specs=[row, row, pl.BlockSpec((d, d), lambda i: (0, 0)), vecs[0][1],
                  pl.BlockSpec((1, d), lambda i: (0, 0)), vecs[1][1], vecs[2][1]],
        out_specs=[row, row],
        out_shape=[jax.ShapeDtypeStruct((t, d), F32), jax.ShapeDtypeStruct((t, d), BF16)],
        compiler_params=_params("arbitrary"),
        name="outproj",
    )(mixed, x, w_out, vecs[0][0], g2.reshape(1, d), vecs[1][0], vecs[2][0])


CONV_TAIL = 8


def _conv_gate(a, a1, a2, u, cw_ref, cb_ref):
    conv = cb_ref[...] + cw_ref[0:1, :] * a2 + cw_ref[1:2, :] * a1 + cw_ref[2:3, :] * a
    return _silu(conv) * u


def _up_matmuls(h2_ref, wg_ref, wv_ref, w16_ref):
    @pl.when(pl.program_id(1) == 0)
    def _():
        w16_ref[0] = wg_ref[...].astype(BF16)
        w16_ref[1] = wv_ref[...].astype(BF16)

    h2 = h2_ref[...]
    return _dot(h2, w16_ref[0]), _dot(h2, w16_ref[1])


def _up_carry_kernel(h2_ref, wg_ref, wv_ref, cw_ref, cb_ref, g_ref, tail_ref, w16_ref, carry_ref, *, tiles_per_seq):
    i = pl.program_id(1)

    @pl.when(i % tiles_per_seq == 0)
    def _():
        carry_ref[...] = jnp.zeros_like(carry_ref)

    a, u = _up_matmuls(h2_ref, wg_ref, wv_ref, w16_ref)
    tm = a.shape[0]
    prev = carry_ref[...]
    head_row = lax.broadcasted_iota(jnp.int32, prev.shape, 0)
    shifted = []
    for shift in (1, 2):
        r = pltpu.roll(a, shift, 0)
        top = jnp.where(head_row < shift, pltpu.roll(prev, shift, 0), r[:CONV_TAIL])
        shifted.append(jnp.concatenate([top, r[CONV_TAIL:]], axis=0))
    tail = a[tm - CONV_TAIL:]
    carry_ref[...] = tail
    tail_ref[...] = tail
    g_ref[...] = _conv_gate(a, shifted[0], shifted[1], u, cw_ref, cb_ref).astype(g_ref.dtype)


def _up_state_kernel(h2_ref, wg_ref, wv_ref, cw_ref, cb_ref, e1_ref, e2_ref, g_ref, a_ref, w16_ref, *, seq_len):
    a, u = _up_matmuls(h2_ref, wg_ref, wv_ref, w16_ref)
    pos = lax.broadcasted_iota(jnp.int32, a.shape, 0) % seq_len
    a1 = jnp.where(pos < 1, e1_ref[...], pltpu.roll(a, 1, 0))
    a2 = jnp.where(pos < 2, e2_ref[...], pltpu.roll(a, 2, 0))
    a_ref[...] = a
    g_ref[...] = _conv_gate(a, a1, a2, u, cw_ref, cb_ref).astype(g_ref.dtype)


def _up_call(rows, h2, w_up, conv_w, conv_b, conv_state, tn=512):
    t, d = h2.shape
    f = w_up.shape[1] // 2
    n_col = f // tn
    tm = min(rows.tm, 512)
    common_in = [pl.BlockSpec((tm, d), lambda j, i: (i, 0)),
                 pl.BlockSpec((d, tn), lambda j, i: (0, j)),
                 pl.BlockSpec((d, tn), lambda j, i: (0, j + n_col)),
                 pl.BlockSpec(conv_w.shape[:1] + (tn,), lambda j, i: (0, j)),
                 pl.BlockSpec((1, tn), lambda j, i: (0, j))]
    tile = pl.BlockSpec((tm, tn), lambda j, i: (i, j))
    g_shape = jax.ShapeDtypeStruct((t, f), BF16)
    args = (h2, w_up, w_up, conv_w, conv_b.reshape(1, f))
    if rows.seq_len >= tm:
        tiles_per_seq = rows.seq_len // tm
        g, tail = pl.pallas_call(
            functools.partial(_up_carry_kernel, tiles_per_seq=tiles_per_seq),
            grid=(n_col, t // tm),
            in_specs=common_in,
            out_specs=[tile, pl.BlockSpec((None, CONV_TAIL, tn), lambda j, i: (i // tiles_per_seq, 0, j))],
            out_shape=[g_shape, jax.ShapeDtypeStruct((rows.n_seq, CONV_TAIL, f), F32)],
            scratch_shapes=[pltpu.VMEM((2, d, tn), BF16), pltpu.VMEM((CONV_TAIL, tn), F32)],
            compiler_params=_params("arbitrary", "arbitrary"),
            name="up_carry",
        )(*args)
        return g, tail[:, CONV_TAIL - 2:, :]
    n_seq, seq_len = rows.n_seq, rows.seq_len
    zeros = jnp.zeros((n_seq, seq_len - 1, f), F32)
    e1 = jnp.concatenate([conv_state[:, 1:2], zeros], axis=1).reshape(t, f)
    e2 = jnp.concatenate([conv_state[:, 0:2], zeros[:, 1:]], axis=1).reshape(t, f)
    g, a = pl.pallas_call(
        functools.partial(_up_state_kernel, seq_len=seq_len),
        grid=(n_col, t // tm),
        in_specs=common_in + [tile, tile],
        out_specs=[tile, tile],
        out_shape=[g_shape, jax.ShapeDtypeStruct((t, f), F32)],
        scratch_shapes=[pltpu.VMEM((2, d, tn), BF16)],
        compiler_params=_params("arbitrary", "arbitrary"),
        name="up_state",
    )(*args, e1, e2)
    return g, a.reshape(n_seq, seq_len, f)[:, seq_len - 2:, :]


def _down_kernel(g_ref, w_ref, x_ref, gt_ref, o_ref):
    o_ref[...] = x_ref[...] + gt_ref[...] * _dot(g_ref[...], w_ref[...])


def _down_call(rows, g, w_down, x1, gt, tn=512):
    t, f = g.shape
    d = w_down.shape[1]
    rows = rows._replace(tm=min(rows.tm, 512))
    tm = rows.tm
    gt_a, _ = _seq_vec(rows, gt, lambda i, j: i)
    if rows.per_row:
        gt_s = pl.BlockSpec((tm, tn), lambda i, j: (i, j))
    else:
        tiles_per_seq = rows.seq_len // tm
        gt_s = pl.BlockSpec((None, 1, tn), lambda i, j: (i // tiles_per_seq, 0, j))
    tile = pl.BlockSpec((tm, tn), lambda i, j: (i, j))
    return pl.pallas_call(
        _down_kernel,
        grid=(t // tm, d // tn),
        in_specs=[pl.BlockSpec((tm, f), lambda i, j: (i, 0)),
                  pl.BlockSpec((f, tn), lambda i, j: (0, j)), tile, gt_s],
        out_specs=tile,
        out_shape=jax.ShapeDtypeStruct((t, d), F32),
        compiler_params=_params("arbitrary", "arbitrary"),
        name="down",
    )(g, w_down, x1, gt_a)


class _Weights(NamedTuple):
    norm1_g: jax.Array
    norm2_g: jax.Array
    w_q: jax.Array
    w_k: jax.Array
    w_v: jax.Array
    w_gqk: jax.Array
    w_gv: jax.Array
    w_gr: jax.Array
    w_gd: jax.Array
    w_msb: jax.Array
    w_mgla: jax.Array
    q_col: jax.Array
    k_col: jax.Array
    gqk_col: jax.Array
    sb_bias: jax.Array
    w_gk_up: jax.Array
    b_gk: jax.Array
    gla_onorm_g: jax.Array
    w_br_sb: jax.Array
    w_br_gla: jax.Array
    w_out: jax.Array
    w_up: jax.Array
    conv_w: jax.Array
    conv_b: jax.Array
    w_down: jax.Array


class _Dims(NamedTuple):
    sb_heads: int
    sb_hd: int
    gla_heads: int
    gla_dk: int
    gla_dv: int
    rank: int


def _prep_weights(dims, norm1_g, norm2_g, w_in, q_norm_g, k_norm_g, sb_bias, w_gk_up, b_gk, gla_onorm_g,
                  w_br_sb, w_br_gla, w_out, w_up, conv_w, conv_b, w_down):
    d = w_in.shape[0]
    sbw = dims.sb_heads * dims.sb_hd
    kw = dims.gla_heads * dims.gla_dk
    vw = dims.gla_heads * dims.gla_dv
    sizes = (sbw, sbw, sbw, 2 * kw, vw, vw, dims.rank, d, d)
    offs = np.concatenate([[0], np.cumsum(sizes)])
    w_q, w_k, w_v, w_gqk, w_gv, w_gr, w_gd, w_msb, w_mgla = (
        w_in[:, offs[n]:offs[n + 1]].astype(BF16) for n in range(len(sizes)))
    w_gd = jnp.pad(w_gd, ((0, 0), (0, LANES - dims.rank)))
    return _Weights(
        norm1_g, norm2_g, w_q, w_k, w_v, w_gqk, w_gv, w_gr, w_gd, w_msb, w_mgla,
        q_col=jnp.tile(q_norm_g, dims.sb_heads) * (dims.sb_hd ** -0.5 * LOG2E),
        k_col=jnp.tile(k_norm_g, dims.sb_heads),
        gqk_col=jnp.concatenate([jnp.full((kw,), dims.gla_dk ** -0.5, F32), jnp.ones((kw,), F32)]),
        sb_bias=sb_bias * LOG2E, w_gk_up=w_gk_up, b_gk=b_gk, gla_onorm_g=gla_onorm_g,
        w_br_sb=w_br_sb.astype(BF16), w_br_gla=w_br_gla.astype(BF16), w_out=w_out.astype(BF16),
        w_up=w_up, conv_w=conv_w, conv_b=conv_b, w_down=w_down.astype(BF16))


def _layer_group(dims, w, rows, x, mod, gla_s0, conv_state, paged):
    n_seq, seq_len, d = x.shape
    t = n_seq * seq_len
    x2 = x.reshape(t, d)
    sh1, sc1, gt1, sh2, sc2, gt2 = jnp.split(mod, 6, axis=-1)
    h = _prenorm_call(rows, x2, w.norm1_g, sc1, sh1)

    (q,) = _proj_call(rows, h, w.w_q, (BF16,), col=w.q_col, head_dim=dims.sb_hd, name="proj_q")
    k32, k16 = _proj_call(rows, h, w.w_k, (F32, BF16), col=w.k_col, head_dim=dims.sb_hd, name="proj_k")
    v32, v16 = _proj_call(rows, h, w.w_v, (F32, BF16), name="proj_v")
    (gqk,) = _proj_call(rows, h, w.w_gqk, (F32,), col=w.gqk_col, name="proj_gqk")
    (gv,) = _proj_call(rows, h, w.w_gv, (BF16,), name="proj_gv")
    (gr,) = _proj_call(rows, h, w.w_gr, (F32,), name="proj_gr")
    (gd,) = _proj_call(rows, h, w.w_gd, (F32,), name="proj_gd")

    sbw = dims.sb_heads * dims.sb_hd
    if paged is None:
        o_sb = _sb_prompt_call(q, k16, v16, w.sb_bias, n_seq, seq_len, dims.sb_hd)
    else:
        cache_k, cache_v, page_table = paged
        n_pool, page = cache_k.shape[:2]
        channel_major = lambda c: jnp.transpose(c, (0, 2, 3, 1)).reshape(n_pool, sbw, page)
        o_sb = _sb_sample_call(q.reshape(n_seq, seq_len, sbw), k32.reshape(n_seq, seq_len, sbw),
                               v32.reshape(n_seq, seq_len, sbw), channel_major(cache_k), channel_major(cache_v),
                               page_table, w.sb_bias, dims.sb_heads, dims.sb_hd).reshape(t, sbw)

    o_gla, s_new = _gla_call(gqk.reshape(n_seq, seq_len, -1), gv.reshape(n_seq, seq_len, -1),
                             gr.reshape(n_seq, seq_len, -1), gd.reshape(n_seq, seq_len, -1),
                             w.w_gk_up, w.b_gk, w.gla_onorm_g, gla_s0, chunk=128 if seq_len >= 128 else 8,
                             rows_per_step=min(256, seq_len))
    o_gla = o_gla.reshape(t, -1)

    mixed = _merge_call(rows, h, o_sb, o_gla, w.w_msb, w.w_mgla, w.w_br_sb, w.w_br_gla)
    x1, h2 = _outproj_call(rows, mixed, x2, w.w_out, gt1, w.norm2_g, sc2, sh2)
    g, conv_new = _up_call(rows, h2, w.w_up, w.conv_w, w.conv_b, conv_state)
    y = _down_call(rows, g, w.w_down, x1, gt2)
    return (y.reshape(n_seq, seq_len, d), k32.reshape(n_seq, seq_len, dims.sb_heads, dims.sb_hd),
            v32.reshape(n_seq, seq_len, dims.sb_heads, dims.sb_hd), s_new, conv_new)


def kernel(x_prompt, x_sample, c_prompt, c_sample, cache_k_sb, cache_v_sb, page_table, state_gla, state_ffn_conv, norm1_g, norm2_g, w_mod, b_mod, w_in, q_norm_g, k_norm_g, sb_bias, w_gk_up, b_gk, gla_onorm_g, w_br_sb, w_br_gla, w_out, w_up, conv_w, conv_b, w_down):
    depth = w_in.shape[0]
    bp, lp, d = x_prompt.shape
    bs, ls, _ = x_sample.shape
    dims = _Dims(sb_heads=cache_k_sb.shape[3], sb_hd=cache_k_sb.shape[4], gla_heads=state_gla.shape[2],
                 gla_dk=state_gla.shape[3], gla_dv=state_gla.shape[4], rank=w_gk_up.shape[1])
    rows_p = _Rows(bp, lp, min(1024, lp))
    rows_s = _Rows(bs, ls, bs * ls)
    pad_rows = -(bp + bs) % 8
    c_all = jnp.concatenate([c_prompt, c_sample, jnp.zeros((pad_rows, d), F32)], axis=0)
    xp, xs = x_prompt, x_sample
    outs = [[] for _ in range(8)]
    for l in range(depth):
        w = _prep_weights(dims, norm1_g[l], norm2_g[l], w_in[l], q_norm_g[l], k_norm_g[l], sb_bias[l], w_gk_up[l],
                          b_gk[l], gla_onorm_g[l], w_br_sb[l], w_br_gla[l], w_out[l], w_up[l], conv_w[l],
                          conv_b[l], w_down[l])
        mod = _mod_call(c_all, w_mod[l], b_mod[l])
        zero_state = jnp.zeros((bp,) + state_gla.shape[2:], F32)
        xp, kp, vp, sp, cp = _layer_group(dims, w, rows_p, xp, mod[:bp], zero_state, None, None)
        xs, ks, vs, ss, cs = _layer_group(dims, w, rows_s, xs, mod[bp:bp + bs], state_gla[l], state_ffn_conv[l],
                                          (cache_k_sb[l], cache_v_sb[l], page_table))
        for lst, val in zip(outs, (kp, vp, sp, cp, ks, vs, ss, cs)):
            lst.append(val)
    return (xp, xs) + tuple(jnp.stack(lst) for lst in outs)
```

```python
import functools
import math
from typing import NamedTuple

import jax
import jax.numpy as jnp
import numpy as np
from jax import lax
from jax.experimental import pallas as pl
from jax.experimental.pallas import tpu as pltpu

F32 = jnp.float32
BF16 = jnp.bfloat16
EPS = 1e-6
GLA_TAU = 16.0
LOG2E = 1.4426950408889634
VMEM_LIMIT_BYTES = 48 * 1024 * 1024
INPROJ_VMEM_LIMIT_BYTES = 56 * 1024 * 1024
LANES = 128
NT_DIMS = (((1,), (1,)), ((), ()))
TN_DIMS = (((0,), (0,)), ((), ()))


def _params(*sem):
    return pltpu.CompilerParams(dimension_semantics=sem, vmem_limit_bytes=VMEM_LIMIT_BYTES)


def _dot(a, b):
    return jnp.dot(a, b, preferred_element_type=F32)


def _log_sigmoid_neg(z):
    return -(jnp.maximum(z, 0.0) + jnp.log(1.0 + jnp.exp(-jnp.abs(z))))


def _softplus2(z):
    return jnp.maximum(z, 0.0) + jnp.log(1.0 + jnp.exp2(-jnp.abs(z))) * LOG2E


def _silu(x):
    return x * jax.nn.sigmoid(x)


class _Rows(NamedTuple):
    n_seq: int
    seq_len: int
    tm: int

    @property
    def total(self):
        return self.n_seq * self.seq_len

    @property
    def per_row(self):
        return self.seq_len < self.tm


def _seq_vec(rows, v, row_tile_of):
    d = v.shape[1]
    if rows.per_row:
        return jnp.repeat(v, rows.seq_len, axis=0), pl.BlockSpec((rows.tm, d), lambda *g: (row_tile_of(*g), 0))
    tiles_per_seq = rows.seq_len // rows.tm
    return v[:, None, :], pl.BlockSpec((None, 1, d), lambda *g: (row_tile_of(*g) // tiles_per_seq, 0, 0))


def _mod_kernel(c_ref, w_ref, b_ref, o_ref):
    c = c_ref[...]
    o_ref[...] = _dot(_silu(c).astype(BF16), w_ref[...].astype(BF16)) + b_ref[...]


def _mod_call(c, w_mod, b_mod):
    r, d = c.shape
    n = w_mod.shape[1]
    tn = 512
    return pl.pallas_call(
        _mod_kernel,
        grid=(n // tn,),
        in_specs=[pl.BlockSpec((r, d), lambda j: (0, 0)),
                  pl.BlockSpec((d, tn), lambda j: (0, j)),
                  pl.BlockSpec((1, tn), lambda j: (0, j))],
        out_specs=pl.BlockSpec((r, tn), lambda j: (0, j)),
        out_shape=jax.ShapeDtypeStruct((r, n), F32),
        compiler_params=_params("arbitrary"),
        name="mod",
    )(c, w_mod, b_mod.reshape(1, n))


def _rms_modulate(x, g, sc, sh):
    y = x * lax.rsqrt(jnp.mean(x * x, axis=-1, keepdims=True) + EPS) * g
    return y * (1.0 + sc) + sh


def _prenorm_kernel(x_ref, g_ref, sc_ref, sh_ref, o_ref):
    o_ref[...] = _rms_modulate(x_ref[...], g_ref[...], sc_ref[...], sh_ref[...]).astype(o_ref.dtype)


def _prenorm_call(rows, x, g, sc, sh):
    t, d = x.shape
    sc_a, sc_s = _seq_vec(rows, sc, lambda i: i)
    sh_a, sh_s = _seq_vec(rows, sh, lambda i: i)
    return pl.pallas_call(
        _prenorm_kernel,
        grid=(t // rows.tm,),
        in_specs=[pl.BlockSpec((rows.tm, d), lambda i: (i, 0)),
                  pl.BlockSpec((1, d), lambda i: (0, 0)), sc_s, sh_s],
        out_specs=pl.BlockSpec((rows.tm, d), lambda i: (i, 0)),
        out_shape=jax.ShapeDtypeStruct((t, d), BF16),
        compiler_params=_params("arbitrary"),
        name="prenorm",
    )(x, g.reshape(1, d), sc_a, sh_a)


class _Seg(NamedTuple):
    first_tile: int
    n_tiles: int
    head_norm: bool
    scale: bool
    outs: tuple


def _inproj_kernel(h_ref, w_ref, wd_ref, col_ref, p_ref, *o_refs, segs, head_dim):
    j = pl.program_id(1)
    h = h_ref[...]
    acc = lax.dot_general(h, w_ref[...].astype(BF16), NT_DIMS, preferred_element_type=F32)

    def head_normalised(y):
        pw = p_ref.shape[0]
        parts = []
        for s in range(y.shape[1] // pw):
            a = y[:, s * pw:(s + 1) * pw]
            ss = _dot((a * a).astype(BF16), p_ref[...])
            parts.append(a * lax.rsqrt(ss * (1.0 / head_dim) + EPS))
        return jnp.concatenate(parts, axis=1)

    refs = iter(o_refs)
    for seg in segs:
        seg_refs = [next(refs) for _ in seg.outs]

        @pl.when((j >= seg.first_tile) & (j < seg.first_tile + seg.n_tiles))
        def _(seg=seg, seg_refs=seg_refs):
            y = head_normalised(acc) if seg.head_norm else acc
            if seg.scale:
                y = y * col_ref[...]
            for o_ref, (dtype, channel_major) in zip(seg_refs, seg.outs):
                o_ref[...] = (y.T if channel_major else y).astype(dtype)

    gd_ref = next(refs)

    @pl.when(j == 0)
    def _():
        gd_ref[...] = lax.dot_general(h, wd_ref[...].astype(BF16), NT_DIMS, preferred_element_type=F32)


def _inproj_call(rows, dims, h, w_in_t, col, channel_major_kv, tn=512):
    t, d = h.shape
    tm = rows.tm
    sbw = dims.sb_heads * dims.sb_hd
    kw2 = 2 * dims.gla_heads * dims.gla_dk
    vw = dims.gla_heads * dims.gla_dv
    f32_kv = (F32, channel_major_kv)
    layout = [(sbw, True, True, ((BF16, False),)), (sbw, True, True, (f32_kv, (BF16, False))),
              (sbw, False, False, (f32_kv, (BF16, False))), (kw2, False, True, ((F32, False),)),
              (vw, False, False, ((BF16, False),)), (vw, False, False, ((F32, False),))]
    segs, widths, tile = [], [], 0
    for width, head_norm, scale, outs in layout:
        segs.append(_Seg(tile, width // tn, head_norm, scale, outs))
        widths.append(width)
        tile += width // tn
    n_tiles = tile
    gd_block = n_tiles * tn // LANES
    tiles_per_seq = max(rows.seq_len // tm, 1)

    out_specs, out_shapes = [], []
    for seg, width in zip(segs, widths):
        local = lambda j, seg=seg: jnp.clip(j - seg.first_tile, 0, seg.n_tiles - 1)
        for dtype, channel_major in seg.outs:
            if channel_major:
                out_specs.append(pl.BlockSpec((None, tn, tm), lambda i, j, local=local: (
                    i // tiles_per_seq, local(j), i % tiles_per_seq)))
                out_shapes.append(jax.ShapeDtypeStruct((rows.n_seq, width, rows.seq_len), dtype))
            else:
                out_specs.append(pl.BlockSpec((tm, tn), lambda i, j, local=local: (i, local(j))))
                out_shapes.append(jax.ShapeDtypeStruct((t, width), dtype))
    out_specs.append(pl.BlockSpec((tm, LANES), lambda i, j: (i, 0)))
    out_shapes.append(jax.ShapeDtypeStruct((t, LANES), F32))

    pw = 256
    head_of = np.arange(pw) // dims.sb_hd
    return pl.pallas_call(
        functools.partial(_inproj_kernel, segs=tuple(segs), head_dim=dims.sb_hd),
        grid=(t // tm, n_tiles),
        in_specs=[pl.BlockSpec((tm, d), lambda i, j: (i, 0)),
                  pl.BlockSpec((tn, d), lambda i, j: (j, 0)),
                  pl.BlockSpec((LANES, d), lambda i, j: (gd_block, 0)),
                  pl.BlockSpec((1, tn), lambda i, j: (0, j)),
                  pl.BlockSpec((pw, pw), lambda i, j: (0, 0))],
        out_specs=out_specs,
        out_shape=out_shapes,
        compiler_params=pltpu.CompilerParams(dimension_semantics=("arbitrary", "arbitrary"),
                                             vmem_limit_bytes=INPROJ_VMEM_LIMIT_BYTES),
        name="inproj",
    )(h, w_in_t, w_in_t, col.reshape(1, -1), jnp.asarray(head_of[:, None] == head_of[None, :], BF16))


def _sb_prompt_kernel(bias_ref, q_ref, k_ref, v_ref, u_ref, o_ref, *, tq, hd, n_pairs):
    group = pl.program_id(1)
    i = pl.program_id(2)
    pw = 2 * hd
    lane = lax.broadcasted_iota(jnp.int32, (tq, pw), 1)
    row = lax.broadcasted_iota(jnp.int32, (2 * tq, tq), 0)
    col = lax.broadcasted_iota(jnp.int32, (2 * tq, tq), 1)
    causal = col < jnp.where(row >= tq, row - tq, row)
    zero = jnp.zeros((tq, pw), q_ref.dtype)
    q2s, biases = [], []
    for p in range(n_pairs):
        q = q_ref[:, p * pw:(p + 1) * pw]
        q2s.append(jnp.concatenate([jnp.where(lane < hd, q, zero), jnp.where(lane >= hd, q, zero)], axis=0))
        head = 2 * (group * n_pairs + p)
        biases.append((bias_ref[head], bias_ref[head + 1]))

    def block(n, carry, masked):
        off = pl.multiple_of((i - n) * tq, tq)
        out = []
        for p, (acc, c) in enumerate(carry):
            lanes = slice(p * pw, (p + 1) * pw)
            z = lax.dot_general(q2s[p], k_ref[pl.ds(off, tq), lanes], NT_DIMS, preferred_element_type=F32)
            z = jnp.concatenate([z[:tq] + biases[p][0], z[tq:] + biases[p][1]], axis=0)
            sp = _softplus2(z)
            if masked:
                sp = jnp.where(causal, sp, 0.0)
            a = jnp.exp2(z + c + _dot(sp.astype(BF16), u_ref[...]))
            if masked:
                a = jnp.where(causal, a, 0.0)
            acc = acc + _dot(a.astype(BF16), v_ref[pl.ds(off, tq), lanes])
            out.append((acc, c - jnp.sum(sp, axis=-1, keepdims=True)))
        return tuple(out)

    init = tuple((jnp.zeros((2 * tq, pw), F32), jnp.zeros((2 * tq, 1), F32)) for _ in range(n_pairs))
    carry = lax.fori_loop(1, i + 1, lambda t, cr: block(t, cr, False), block(0, init, True))
    for p, (acc, _) in enumerate(carry):
        o_ref[:, p * pw:(p + 1) * pw] = jnp.where(lane < hd, acc[:tq], acc[tq:]).astype(o_ref.dtype)


def _sb_prompt_call(q, k, v, bias, n_seq, seq_len, hd, tq=256, n_pairs=8):
    t, width = q.shape
    gw = 2 * hd * n_pairs
    nq = seq_len // tq
    u = jnp.asarray(-np.tril(np.ones((tq, tq), np.float32)), BF16)
    return pl.pallas_call(
        functools.partial(_sb_prompt_kernel, tq=tq, hd=hd, n_pairs=n_pairs),
        grid=(n_seq, width // gw, nq),
        in_specs=[pl.BlockSpec(memory_space=pltpu.SMEM),
                  pl.BlockSpec((tq, gw), lambda b, g, i: (b * nq + i, g)),
                  pl.BlockSpec((seq_len, gw), lambda b, g, i: (b, g)),
                  pl.BlockSpec((seq_len, gw), lambda b, g, i: (b, g)),
                  pl.BlockSpec((tq, tq), lambda b, g, i: (0, 0))],
        out_specs=pl.BlockSpec((tq, gw), lambda b, g, i: (b * nq + i, g)),
        out_shape=jax.ShapeDtypeStruct((t, width), BF16),
        compiler_params=_params("arbitrary", "arbitrary", "arbitrary"),
        name="sb_prompt",
    )(bias, q, k, v, u)


def _sb_sample_kernel(pt_ref, qbd_ref, bias_ref, kn_ref, vn_ref, *rest, n_slots, n_heads, hd, page):
    del pt_ref
    kp_refs = rest[:n_slots]
    vp_refs = rest[n_slots:2 * n_slots]
    u_ref, o_ref, acc_ref, c_ref = rest[2 * n_slots:]
    s = pl.program_id(1)
    qbd = qbd_ref[...]
    n_rows, width = qbd.shape
    n_new = kn_ref.shape[0]

    def block(z, mask, weigh):
        n_keys = z.shape[1]
        gsz = min(n_keys, u_ref.shape[0])
        z = z + bias_ref[...]
        sp = _softplus2(z)
        if mask is not None:
            sp = jnp.where(mask, sp, 0.0)
        c = c_ref[...]
        parts = []
        for g0 in range(0, n_keys, gsz):
            spg = sp[:, g0:g0 + gsz]
            incl = _dot(spg.astype(BF16), u_ref[:gsz, :gsz])
            parts.append(jnp.exp2(z[:, g0:g0 + gsz] + incl + jnp.concatenate([c] * (gsz // c.shape[1]), axis=1)))
            c = c - jnp.sum(spg, axis=-1, keepdims=True)
        a = parts[0] if len(parts) == 1 else jnp.concatenate(parts, axis=1)
        if mask is not None:
            a = jnp.where(mask, a, 0.0)
        acc_ref[...] += weigh(a.astype(BF16))
        c_ref[...] = c

    @pl.when(s == 0)
    def _():
        acc_ref[...] = jnp.zeros_like(acc_ref)
        c_ref[...] = jnp.zeros_like(c_ref)
        pad = jnp.zeros((page - n_new, width), F32)
        kb = jnp.concatenate([kn_ref[...], pad], axis=0).astype(BF16)
        vb = jnp.concatenate([vn_ref[...], pad], axis=0).astype(BF16)
        row = lax.broadcasted_iota(jnp.int32, (n_rows, page), 0)
        col = lax.broadcasted_iota(jnp.int32, (n_rows, page), 1)
        mask = col * n_heads < row - row % n_heads
        block(lax.dot_general(qbd, kb, NT_DIMS, preferred_element_type=F32), mask, lambda a: _dot(a, vb))

    @pl.when(s > 0)
    def _():
        kt = jnp.concatenate([r[...].astype(BF16) for r in kp_refs], axis=1)
        vt = jnp.concatenate([r[...].astype(BF16) for r in vp_refs], axis=1)
        block(_dot(qbd, kt), None, lambda a: lax.dot_general(a, vt, NT_DIMS, preferred_element_type=F32))

    @pl.when(s == pl.num_programs(1) - 1)
    def _():
        row = lax.broadcasted_iota(jnp.int32, (n_rows, width), 0)
        col = lax.broadcasted_iota(jnp.int32, (n_rows, width), 1)
        own = jnp.where(col // hd == row % n_heads, acc_ref[...], 0.0)
        o_ref[...] = jnp.sum(own.reshape(n_rows // n_heads, n_heads, width), axis=1).astype(o_ref.dtype)


def _sb_sample_call(q, k_new, v_new, cache_k, cache_v, page_table, bias, n_heads, hd, n_slots=8):
    bsz, n_q, width = q.shape
    page = cache_k.shape[2]
    n_pages = page_table.shape[1]
    n_steps = n_pages // n_slots
    n_rows = n_q * n_heads
    n_new = 8
    eye = jnp.eye(n_heads, dtype=q.dtype)
    qbd = jnp.einsum("bqhd,hg->bqhgd", q.reshape(bsz, n_q, n_heads, hd), eye).reshape(bsz, n_rows, width)
    bias_rows = jnp.tile(bias.astype(F32), n_q).reshape(n_rows, 1)
    pad = ((0, 0), (0, n_new - n_q), (0, 0))
    k_new, v_new = jnp.pad(k_new, pad), jnp.pad(v_new, pad)
    slot_of, pos_of = np.divmod(np.arange(min(n_slots, 4) * page), page)
    newer_or_same = (slot_of[:, None] < slot_of[None, :]) | (
        (slot_of[:, None] == slot_of[None, :]) & (pos_of[:, None] >= pos_of[None, :]))
    u = jnp.asarray(-newer_or_same.astype(np.float32), BF16)

    def page_spec(slot):
        def imap(b, s, pt):
            return pt[b, n_pages - 1 - (jnp.maximum(s, 1) - 1) * n_slots - slot], 0, 0
        return pl.BlockSpec((None, width, page), imap)

    per_seq = lambda b, s, pt: (b, 0, 0)
    const = lambda b, s, pt: (0, 0)
    grid_spec = pltpu.PrefetchScalarGridSpec(
        num_scalar_prefetch=1,
        grid=(bsz, n_steps + 1),
        in_specs=[pl.BlockSpec((None, n_rows, width), per_seq),
                  pl.BlockSpec((n_rows, 1), const),
                  pl.BlockSpec((None, n_new, width), per_seq),
                  pl.BlockSpec((None, n_new, width), per_seq)]
                 + [page_spec(slot) for slot in range(n_slots)] * 2
                 + [pl.BlockSpec(u.shape, const)],
        out_specs=pl.BlockSpec((None, n_q, width), per_seq),
        scratch_shapes=[pltpu.VMEM((n_rows, width), F32), pltpu.VMEM((n_rows, LANES), F32)],
    )
    return pl.pallas_call(
        functools.partial(_sb_sample_kernel, n_slots=n_slots, n_heads=n_heads, hd=hd, page=page),
        grid_spec=grid_spec,
        out_shape=jax.ShapeDtypeStruct((bsz, n_q, width), BF16),
        compiler_params=_params("arbitrary", "arbitrary"),
        name="sb_sample",
    )(page_table, qbd, bias_rows, k_new, v_new, *([cache_k] * n_slots), *([cache_v] * n_slots), u)


def _gla_tables(c):
    n_lev = int(math.log2(c))
    assert 1 << n_lev == c
    level = np.full((c, c), -1, np.int32)
    sign = np.zeros((n_lev, c, LANES), np.float32)
    idx = np.arange(c)
    for l in range(n_lev):
        half = c >> (l + 1)
        start = idx // (2 * half) * (2 * half)
        bound = start + half
        same = start[:, None] == start[None, :]
        level[same & (idx[:, None] >= bound[:, None]) & (idx[None, :] < bound[None, :])] = l
        sign[l] = np.where(idx >= bound, 1.0, -1.0)[:, None]
    level[idx, idx] = n_lev
    return np.tril(np.ones((c, c), np.float32)), level, sign.reshape(n_lev * c, LANES), n_lev


def _boundary_rows(b, half):
    c, w = b.shape
    sublanes = 8
    if 2 * half >= sublanes:
        return jnp.concatenate([jnp.broadcast_to(b[s + half - 1:s + half], (2 * half, w))
                                for s in range(0, c, 2 * half)], axis=0)
    b3 = b.reshape(c // sublanes, sublanes, w)
    sub = lax.broadcasted_iota(jnp.int32, b3.shape, 1)
    out = None
    for s in range(0, sublanes, 2 * half):
        piece = jnp.broadcast_to(b3[:, s + half - 1:s + half, :], b3.shape)
        out = piece if out is None else jnp.where(sub >= s, piece, out)
    return out.reshape(c, w)


def _pad_rows(x, n):
    if x.shape[0] == n:
        return x
    return jnp.concatenate([x, jnp.zeros((n - x.shape[0],) + x.shape[1:], x.dtype)], axis=0)


def _split3(x):
    hi = x.astype(BF16)
    r = x - hi.astype(F32)
    mid = r.astype(BF16)
    lo = (r - mid.astype(F32)).astype(BF16)
    return hi, mid, lo


def _gla_kernel(q_ref, k_ref, v_ref, gr_ref, gd_ref, wup_ref, bgk_ref, gno_ref, s0_ref, tri_ref, lvl_ref, sgn_ref,
                o_ref, sout_ref, s_ref, *, chunk, n_lev, rank):
    rb = pl.program_id(1)

    @pl.when(rb == 0)
    def _():
        s_ref[...] = s0_ref[...]

    n_rows = q_ref.shape[0]
    n_heads, dk, dv = s_ref.shape
    level = lvl_ref[...]
    tri = tri_ref[...]
    ones = jnp.ones((chunk, dk), BF16)
    for start in range(0, n_rows, chunk):
        valid = min(chunk, n_rows - start)
        rows = slice(start, start + valid)
        z = _dot(_pad_rows(gd_ref[rows, :rank], chunk).astype(BF16), wup_ref[...]) + bgk_ref[...]
        g = _log_sigmoid_neg(-z) * (1.0 / GLA_TAU)
        if valid < chunk:
            g = jnp.where(lax.broadcasted_iota(jnp.int32, g.shape, 0) < valid, g, 0.0)
        g3 = _split3(g)
        b = sum(_dot(tri, gp) for gp in g3)
        q = _pad_rows(q_ref[rows, :], chunk)
        k = _pad_rows(k_ref[rows, :], chunk)
        factored = [(q.astype(BF16), k.astype(BF16))]
        for l in range(n_lev):
            sgn = jnp.concatenate([sgn_ref[l * chunk:(l + 1) * chunk, :]] * (q.shape[1] // LANES), axis=1)
            x = jnp.exp((b - _boundary_rows(b, chunk >> (l + 1))) * sgn)
            factored.append(((q * x).astype(BF16), (k * x).astype(BF16)))
        qb = (q * jnp.exp(b)).astype(BF16)
        kb = (k * jnp.exp(b[chunk - 1:chunk] - b)).astype(BF16)
        for h in range(n_heads):
            kl = slice(h * dk, (h + 1) * dk)
            vl = slice(h * dv, (h + 1) * dv)
            scores = jnp.zeros((chunk, chunk), F32)
            for l, (ql, kf) in enumerate(factored):
                s_l = lax.dot_general(ql[:, kl], kf[:, kl], NT_DIMS, preferred_element_type=F32)
                scores = jnp.where(level == (l - 1 if l else n_lev), s_l, scores)
            v = _pad_rows(v_ref[rows, vl].astype(F32), chunk).astype(BF16)
            state = s_ref[h]
            o = _dot(scores.astype(BF16), v) + _dot(qb[:, kl], state.astype(BF16))
            d_col = jnp.exp(sum(lax.dot_general(gp[:, kl], ones, TN_DIMS, preferred_element_type=F32) for gp in g3))
            s_ref[h] = (state * jnp.concatenate([d_col] * (dv // dk), axis=1)
                        + lax.dot_general(kb[:, kl], v, TN_DIMS, preferred_element_type=F32))
            on = o * lax.rsqrt(jnp.mean(o * o, axis=-1, keepdims=True) + EPS) * gno_ref[...]
            o_ref[rows, vl] = (on[:valid] * _silu(gr_ref[rows, vl])).astype(o_ref.dtype)

    @pl.when(rb == pl.num_programs(1) - 1)
    def _():
        sout_ref[...] = s_ref[...]


def _gla_call(qk, v, gr, gd, w_gk_up, b_gk, gno, s0, chunk, rows_per_step):
    bsz, seq = qk.shape[:2]
    kw = qk.shape[2] // 2
    n_heads, dk, dv = s0.shape[1:]
    rank = w_gk_up.shape[0]
    tri, level, sign, n_lev = _gla_tables(chunk)
    r = rows_per_step
    blk = lambda w: pl.BlockSpec((None, r, w), lambda b, i: (b, i, 0))
    const2 = lambda b, i: (0, 0)
    full = lambda a: pl.BlockSpec(a.shape, const2)
    state_spec = pl.BlockSpec((None, n_heads, dk, dv), lambda b, i: (b, 0, 0, 0))
    return pl.pallas_call(
        functools.partial(_gla_kernel, chunk=chunk, n_lev=n_lev, rank=rank),
        grid=(bsz, seq // r),
        in_specs=[blk(kw), pl.BlockSpec((None, r, kw), lambda b, i: (b, i, 1)),
                  blk(n_heads * dv), blk(n_heads * dv), blk(gd.shape[2]),
                  pl.BlockSpec((rank, kw), const2), pl.BlockSpec((1, kw), const2), pl.BlockSpec((1, dv), const2),
                  state_spec, full(tri), full(level), full(sign)],
        out_specs=[blk(n_heads * dv), state_spec],
        out_shape=[jax.ShapeDtypeStruct((bsz, seq, n_heads * dv), BF16),
                   jax.ShapeDtypeStruct(s0.shape, F32)],
        scratch_shapes=[pltpu.VMEM((n_heads, dk, dv), F32)],
        compiler_params=_params("arbitrary", "arbitrary"),
        name="gla",
    )(qk, qk, v, gr, gd, w_gk_up.astype(BF16), b_gk.reshape(1, -1), gno.reshape(1, dv), s0,
      jnp.asarray(tri, BF16), jnp.asarray(level), jnp.asarray(sign))


def _merge_kernel(h_ref, osb_ref, ogla_ref, wmsb_ref, wmgla_ref, wbsb_ref, wbgla_ref, o_ref):
    h = h_ref[...]
    gate = lambda w_t_ref: jax.nn.sigmoid(lax.dot_general(h, w_t_ref[...], NT_DIMS, preferred_element_type=F32))
    mixed = (gate(wmsb_ref) * _dot(osb_ref[...], wbsb_ref[...])
             + gate(wmgla_ref) * _dot(ogla_ref[...], wbgla_ref[...]))
    o_ref[...] = mixed.astype(o_ref.dtype)


def _merge_call(rows, h, o_sb, o_gla, w_msb_t, w_mgla_t, w_br_sb, w_br_gla, tn=512):
    t, d = h.shape
    tm = min(rows.tm, 512)
    row = lambda w: pl.BlockSpec((tm, w), lambda i, j: (i, 0))
    wcol = lambda kdim: pl.BlockSpec((kdim, tn), lambda i, j: (0, j))
    wrow = pl.BlockSpec((tn, d), lambda i, j: (j, 0))
    return pl.pallas_call(
        _merge_kernel,
        grid=(t // tm, d // tn),
        in_specs=[row(d), row(o_sb.shape[1]), row(o_gla.shape[1]),
                  wrow, wrow, wcol(o_sb.shape[1]), wcol(o_gla.shape[1])],
        out_specs=pl.BlockSpec((tm, tn), lambda i, j: (i, j)),
        out_shape=jax.ShapeDtypeStruct((t, d), BF16),
        compiler_params=_params("arbitrary", "arbitrary"),
        name="merge",
    )(h, o_sb, o_gla, w_msb_t, w_mgla_t, w_br_sb, w_br_gla)


def _outproj_kernel(mixed_ref, x_ref, w_ref, gt_ref, g_ref, sc_ref, sh_ref, x1_ref, h2_ref):
    x1 = x_ref[...] + gt_ref[...] * _dot(mixed_ref[...], w_ref[...])
    x1_ref[...] = x1
    h2_ref[...] = _rms_modulate(x1, g_ref[...], sc_ref[...], sh_ref[...]).astype(h2_ref.dtype)


def _outproj_call(rows, mixed, x, w_out, gt, g2, sc, sh):
    t, d = x.shape
    rows = rows._replace(tm=min(rows.tm, 256))
    tm = rows.tm
    vecs = [_seq_vec(rows, v, lambda i: i) for v in (gt, sc, sh)]
    row = pl.BlockSpec((tm, d), lambda i: (i, 0))
    return pl.pallas_call(
        _outproj_kernel,
        grid=(t // tm,),
        in_specs=[row, row, pl.BlockSpec((d, d), lambda i: (0, 0)), vecs[0][1],
                  pl.BlockSpec((1, d), lambda i: (0, 0)), vecs[1][1], vecs[2][1]],
        out_specs=[row, row],
        out_shape=[jax.ShapeDtypeStruct((t, d), F32), jax.ShapeDtypeStruct((t, d), BF16)],
        compiler_params=_params("arbitrary"),
        name="outproj",
    )(mixed, x, w_out, vecs[0][0], g2.reshape(1, d), vecs[1][0], vecs[2][0])


CONV_TAIL = 8


def _conv_gate(a, a1, a2, u, cw_ref, cb_ref):
    conv = cb_ref[...] + cw_ref[0:1, :] * a2 + cw_ref[1:2, :] * a1 + cw_ref[2:3, :] * a
    return _silu(conv) * u


def _up_matmuls(h2_ref, wg_ref, wv_ref, w16_ref):
    @pl.when(pl.program_id(1) == 0)
    def _():
        w16_ref[0] = wg_ref[...].astype(BF16)
        w16_ref[1] = wv_ref[...].astype(BF16)

    h2 = h2_ref[...]
    return _dot(h2, w16_ref[0]), _dot(h2, w16_ref[1])


def _up_carry_kernel(h2_ref, wg_ref, wv_ref, cw_ref, cb_ref, g_ref, tail_ref, w16_ref, carry_ref, *, tiles_per_seq):
    i = pl.program_id(1)

    @pl.when(i % tiles_per_seq == 0)
    def _():
        carry_ref[...] = jnp.zeros_like(carry_ref)

    a, u = _up_matmuls(h2_ref, wg_ref, wv_ref, w16_ref)
    tm = a.shape[0]
    prev = carry_ref[...]
    head_row = lax.broadcasted_iota(jnp.int32, prev.shape, 0)
    shifted = []
    for shift in (1, 2):
        r = pltpu.roll(a, shift, 0)
        top = jnp.where(head_row < shift, pltpu.roll(prev, shift, 0), r[:CONV_TAIL])
        shifted.append(jnp.concatenate([top, r[CONV_TAIL:]], axis=0))
    tail = a[tm - CONV_TAIL:]
    carry_ref[...] = tail
    tail_ref[...] = tail
    g_ref[...] = _conv_gate(a, shifted[0], shifted[1], u, cw_ref, cb_ref).astype(g_ref.dtype)


def _up_state_kernel(h2_ref, wg_ref, wv_ref, cw_ref, cb_ref, e1_ref, e2_ref, g_ref, a_ref, w16_ref, *, seq_len):
    a, u = _up_matmuls(h2_ref, wg_ref, wv_ref, w16_ref)
    pos = lax.broadcasted_iota(jnp.int32, a.shape, 0) % seq_len
    a1 = jnp.where(pos < 1, e1_ref[...], pltpu.roll(a, 1, 0))
    a2 = jnp.where(pos < 2, e2_ref[...], pltpu.roll(a, 2, 0))
    a_ref[...] = a
    g_ref[...] = _conv_gate(a, a1, a2, u, cw_ref, cb_ref).astype(g_ref.dtype)


def _up_call(rows, h2, w_up, conv_w, conv_b, conv_state, tn=512):
    t, d = h2.shape
    f = w_up.shape[1] // 2
    n_col = f // tn
    tm = min(rows.tm, 512)
    common_in = [pl.BlockSpec((tm, d), lambda j, i: (i, 0)),
                 pl.BlockSpec((d, tn), lambda j, i: (0, j)),
                 pl.BlockSpec((d, tn), lambda j, i: (0, j + n_col)),
                 pl.BlockSpec(conv_w.shape[:1] + (tn,), lambda j, i: (0, j)),
                 pl.BlockSpec((1, tn), lambda j, i: (0, j))]
    tile = pl.BlockSpec((tm, tn), lambda j, i: (i, j))
    g_shape = jax.ShapeDtypeStruct((t, f), BF16)
    args = (h2, w_up, w_up, conv_w, conv_b.reshape(1, f))
    if rows.seq_len >= tm:
        tiles_per_seq = rows.seq_len // tm
        g, tail = pl.pallas_call(
            functools.partial(_up_carry_kernel, tiles_per_seq=tiles_per_seq),
            grid=(n_col, t // tm),
            in_specs=common_in,
            out_specs=[tile, pl.BlockSpec((None, CONV_TAIL, tn), lambda j, i: (i // tiles_per_seq, 0, j))],
            out_shape=[g_shape, jax.ShapeDtypeStruct((rows.n_seq, CONV_TAIL, f), F32)],
            scratch_shapes=[pltpu.VMEM((2, d, tn), BF16), pltpu.VMEM((CONV_TAIL, tn), F32)],
            compiler_params=_params("arbitrary", "arbitrary"),
            name="up_carry",
        )(*args)
        return g, tail[:, CONV_TAIL - 2:, :]
    n_seq, seq_len = rows.n_seq, rows.seq_len
    zeros = jnp.zeros((n_seq, seq_len - 1, f), F32)
    e1 = jnp.concatenate([conv_state[:, 1:2], zeros], axis=1).reshape(t, f)
    e2 = jnp.concatenate([conv_state[:, 0:2], zeros[:, 1:]], axis=1).reshape(t, f)
    g, a = pl.pallas_call(
        functools.partial(_up_state_kernel, seq_len=seq_len),
        grid=(n_col, t // tm),
        in_specs=common_in + [tile, tile],
        out_specs=[tile, tile],
        out_shape=[g_shape, jax.ShapeDtypeStruct((t, f), F32)],
        scratch_shapes=[pltpu.VMEM((2, d, tn), BF16)],
        compiler_params=_params("arbitrary", "arbitrary"),
        name="up_state",
    )(*args, e1, e2)
    return g, a.reshape(n_seq, seq_len, f)[:, seq_len - 2:, :]


def _down_kernel(g_ref, w_ref, x_ref, gt_ref, o_ref):
    o_ref[...] = x_ref[...] + gt_ref[...] * _dot(g_ref[...], w_ref[...])


def _down_call(rows, g, w_down, x1, gt, tn=512):
    t, f = g.shape
    d = w_down.shape[1]
    rows = rows._replace(tm=min(rows.tm, 512))
    tm = rows.tm
    gt_a, _ = _seq_vec(rows, gt, lambda i, j: i)
    if rows.per_row:
        gt_s = pl.BlockSpec((tm, tn), lambda i, j: (i, j))
    else:
        tiles_per_seq = rows.seq_len // tm
        gt_s = pl.BlockSpec((None, 1, tn), lambda i, j: (i // tiles_per_seq, 0, j))
    tile = pl.BlockSpec((tm, tn), lambda i, j: (i, j))
    return pl.pallas_call(
        _down_kernel,
        grid=(t // tm, d // tn),
        in_specs=[pl.BlockSpec((tm, f), lambda i, j: (i, 0)),
                  pl.BlockSpec((f, tn), lambda i, j: (0, j)), tile, gt_s],
        out_specs=tile,
        out_shape=jax.ShapeDtypeStruct((t, d), F32),
        compiler_params=_params("arbitrary", "arbitrary"),
        name="down",
    )(g, w_down, x1, gt_a)


class _Weights(NamedTuple):
    norm1_g: jax.Array
    norm2_g: jax.Array
    w_in_t: jax.Array
    w_msb_t: jax.Array
    w_mgla_t: jax.Array
    in_col: jax.Array
    sb_bias: jax.Array
    w_gk_up: jax.Array
    b_gk: jax.Array
    gla_onorm_g: jax.Array
    w_br_sb: jax.Array
    w_br_gla: jax.Array
    w_out: jax.Array
    w_up: jax.Array
    conv_w: jax.Array
    conv_b: jax.Array
    w_down: jax.Array


class _Dims(NamedTuple):
    sb_heads: int
    sb_hd: int
    gla_heads: int
    gla_dk: int
    gla_dv: int
    rank: int


def _prep_weights(dims, norm1_g, norm2_g, w_in, q_norm_g, k_norm_g, sb_bias, w_gk_up, b_gk, gla_onorm_g,
                  w_br_sb, w_br_gla, w_out, w_up, conv_w, conv_b, w_down):
    d = w_in.shape[0]
    sbw = dims.sb_heads * dims.sb_hd
    kw = dims.gla_heads * dims.gla_dk
    vw = dims.gla_heads * dims.gla_dv
    gates_at = 3 * sbw + 2 * kw + 2 * vw + dims.rank
    in_col = jnp.concatenate([
        jnp.tile(q_norm_g, dims.sb_heads) * (dims.sb_hd ** -0.5 * LOG2E), jnp.tile(k_norm_g, dims.sb_heads),
        jnp.ones((sbw,), F32), jnp.full((kw,), dims.gla_dk ** -0.5, F32), jnp.ones((kw + 2 * vw,), F32)])
    w_in_t = w_in.T
    return _Weights(
        norm1_g, norm2_g, w_in_t, w_in_t[gates_at:gates_at + d].astype(BF16),
        w_in_t[gates_at + d:gates_at + 2 * d].astype(BF16), in_col,
        sb_bias=sb_bias * LOG2E, w_gk_up=w_gk_up, b_gk=b_gk, gla_onorm_g=gla_onorm_g,
        w_br_sb=w_br_sb.astype(BF16), w_br_gla=w_br_gla.astype(BF16), w_out=w_out.astype(BF16),
        w_up=w_up, conv_w=conv_w, conv_b=conv_b, w_down=w_down.astype(BF16))


def _layer_group(dims, w, rows, x, mod, gla_s0, conv_state, paged):
    n_seq, seq_len, d = x.shape
    t = n_seq * seq_len
    x2 = x.reshape(t, d)
    sh1, sc1, gt1, sh2, sc2, gt2 = jnp.split(mod, 6, axis=-1)
    h = _prenorm_call(rows, x2, w.norm1_g, sc1, sh1)

    q, k32, k16, v32, v16, gqk, gv, gr, gd = _inproj_call(rows, dims, h, w.w_in_t, w.in_col,
                                                          channel_major_kv=paged is None)
    sbw = dims.sb_heads * dims.sb_hd
    heads = (n_seq, seq_len, dims.sb_heads, dims.sb_hd)
    if paged is None:
        o_sb = _sb_prompt_call(q, k16, v16, w.sb_bias, n_seq, seq_len, dims.sb_hd)
        k_out, v_out = (jnp.transpose(a.reshape(n_seq, dims.sb_heads, dims.sb_hd, seq_len), (0, 3, 1, 2))
                        for a in (k32, v32))
    else:
        k_out, v_out = k32.reshape(heads), v32.reshape(heads)
        cache_k, cache_v, page_table = paged
        n_pool, page = cache_k.shape[:2]
        channel_major = lambda c: jnp.transpose(c, (0, 2, 3, 1)).reshape(n_pool, sbw, page)
        o_sb = _sb_sample_call(q.reshape(n_seq, seq_len, sbw), k32.reshape(n_seq, seq_len, sbw),
                               v32.reshape(n_seq, seq_len, sbw), channel_major(cache_k), channel_major(cache_v),
                               page_table, w.sb_bias, dims.sb_heads, dims.sb_hd).reshape(t, sbw)

    o_gla, s_new = _gla_call(gqk.reshape(n_seq, seq_len, -1), gv.reshape(n_seq, seq_len, -1),
                             gr.reshape(n_seq, seq_len, -1), gd.reshape(n_seq, seq_len, -1),
                             w.w_gk_up, w.b_gk, w.gla_onorm_g, gla_s0, chunk=128 if seq_len >= 128 else 8,
                             rows_per_step=min(256, seq_len))
    o_gla = o_gla.reshape(t, -1)

    mixed = _merge_call(rows, h, o_sb, o_gla, w.w_msb_t, w.w_mgla_t, w.w_br_sb, w.w_br_gla)
    x1, h2 = _outproj_call(rows, mixed, x2, w.w_out, gt1, w.norm2_g, sc2, sh2)
    g, conv_new = _up_call(rows, h2, w.w_up, w.conv_w, w.conv_b, conv_state)
    y = _down_call(rows, g, w.w_down, x1, gt2)
    return y.reshape(n_seq, seq_len, d), k_out, v_out, s_new, conv_new


def kernel(x_prompt, x_sample, c_prompt, c_sample, cache_k_sb, cache_v_sb, page_table, state_gla, state_ffn_conv, norm1_g, norm2_g, w_mod, b_mod, w_in, q_norm_g, k_norm_g, sb_bias, w_gk_up, b_gk, gla_onorm_g, w_br_sb, w_br_gla, w_out, w_up, conv_w, conv_b, w_down):
    depth = w_in.shape[0]
    bp, lp, d = x_prompt.shape
    bs, ls, _ = x_sample.shape
    dims = _Dims(sb_heads=cache_k_sb.shape[3], sb_hd=cache_k_sb.shape[4], gla_heads=state_gla.shape[2],
                 gla_dk=state_gla.shape[3], gla_dv=state_gla.shape[4], rank=w_gk_up.shape[1])
    rows_p = _Rows(bp, lp, min(1024, lp))
    rows_s = _Rows(bs, ls, bs * ls)
    pad_rows = -(bp + bs) % 8
    c_all = jnp.concatenate([c_prompt, c_sample, jnp.zeros((pad_rows, d), F32)], axis=0)
    xp, xs = x_prompt, x_sample
    outs = [[] for _ in range(8)]
    for l in range(depth):
        w = _prep_weights(dims, norm1_g[l], norm2_g[l], w_in[l], q_norm_g[l], k_norm_g[l], sb_bias[l], w_gk_up[l],
                          b_gk[l], gla_onorm_g[l], w_br_sb[l], w_br_gla[l], w_out[l], w_up[l], conv_w[l],
                          conv_b[l], w_down[l])
        mod = _mod_call(c_all, w_mod[l], b_mod[l])
        zero_state = jnp.zeros((bp,) + state_gla.shape[2:], F32)
        xp, kp, vp, sp, cp = _layer_group(dims, w, rows_p, xp, mod[:bp], zero_state, None, None)
        xs, ks, vs, ss, cs = _layer_group(dims, w, rows_s, xs, mod[bp:bp + bs], state_gla[l], state_ffn_conv[l],
                                          (cache_k_sb[l], cache_v_sb[l], page_table))
        for lst, val in zip(outs, (kp, vp, sp, cp, ks, vs, ss, cs)):
            lst.append(val)
    return (xp, xs) + tuple(jnp.stack(lst) for lst in outs)
```

```python
import functools
import math
from typing import NamedTuple

import jax
import jax.numpy as jnp
import numpy as np
from jax import lax
from jax.experimental import pallas as pl
from jax.experimental.pallas import tpu as pltpu

F32 = jnp.float32
BF16 = jnp.bfloat16
EPS = 1e-6
GLA_TAU = 16.0
LOG2E = 1.4426950408889634
VMEM_LIMIT_BYTES = 48 * 1024 * 1024
LARGE_VMEM_LIMIT_BYTES = 56 * 1024 * 1024
LANES = 128
NT_DIMS = (((1,), (1,)), ((), ()))
TN_DIMS = (((0,), (0,)), ((), ()))


def _params(*sem):
    return pltpu.CompilerParams(dimension_semantics=sem, vmem_limit_bytes=VMEM_LIMIT_BYTES)


def _dot(a, b):
    return jnp.dot(a, b, preferred_element_type=F32)


def _log_sigmoid_neg(z):
    return -(jnp.maximum(z, 0.0) + jnp.log(1.0 + jnp.exp(-jnp.abs(z))))


def _softplus2(z):
    return jnp.maximum(z, 0.0) + jnp.log(1.0 + jnp.exp2(-jnp.abs(z))) * LOG2E


def _silu(x):
    return x * jax.nn.sigmoid(x)


class _Rows(NamedTuple):
    n_seq: int
    seq_len: int
    tm: int

    @property
    def total(self):
        return self.n_seq * self.seq_len

    @property
    def per_row(self):
        return self.seq_len < self.tm


def _seq_vec(rows, v, row_tile_of):
    d = v.shape[1]
    if rows.per_row:
        return jnp.repeat(v, rows.seq_len, axis=0), pl.BlockSpec((rows.tm, d), lambda *g: (row_tile_of(*g), 0))
    tiles_per_seq = rows.seq_len // rows.tm
    return v[:, None, :], pl.BlockSpec((None, 1, d), lambda *g: (row_tile_of(*g) // tiles_per_seq, 0, 0))


def _mod_kernel(c_ref, w_ref, b_ref, o_ref):
    c = c_ref[...]
    o_ref[...] = _dot(_silu(c).astype(BF16), w_ref[...].astype(BF16)) + b_ref[...]


def _mod_call(c, w_mod, b_mod):
    r, d = c.shape
    n = w_mod.shape[1]
    tn = 512
    return pl.pallas_call(
        _mod_kernel,
        grid=(n // tn,),
        in_specs=[pl.BlockSpec((r, d), lambda j: (0, 0)),
                  pl.BlockSpec((d, tn), lambda j: (0, j)),
                  pl.BlockSpec((1, tn), lambda j: (0, j))],
        out_specs=pl.BlockSpec((r, tn), lambda j: (0, j)),
        out_shape=jax.ShapeDtypeStruct((r, n), F32),
        compiler_params=_params("arbitrary"),
        name="mod",
    )(c, w_mod, b_mod.reshape(1, n))


def _rms_modulate(x, g, sc, sh):
    y = x * lax.rsqrt(jnp.mean(x * x, axis=-1, keepdims=True) + EPS) * g
    return y * (1.0 + sc) + sh


def _prenorm_kernel(x_ref, g_ref, sc_ref, sh_ref, o_ref):
    o_ref[...] = _rms_modulate(x_ref[...], g_ref[...], sc_ref[...], sh_ref[...]).astype(o_ref.dtype)


def _prenorm_call(rows, x, g, sc, sh):
    t, d = x.shape
    sc_a, sc_s = _seq_vec(rows, sc, lambda i: i)
    sh_a, sh_s = _seq_vec(rows, sh, lambda i: i)
    return pl.pallas_call(
        _prenorm_kernel,
        grid=(t // rows.tm,),
        in_specs=[pl.BlockSpec((rows.tm, d), lambda i: (i, 0)),
                  pl.BlockSpec((1, d), lambda i: (0, 0)), sc_s, sh_s],
        out_specs=pl.BlockSpec((rows.tm, d), lambda i: (i, 0)),
        out_shape=jax.ShapeDtypeStruct((t, d), BF16),
        compiler_params=_params("arbitrary"),
        name="prenorm",
    )(x, g.reshape(1, d), sc_a, sh_a)


class _Seg(NamedTuple):
    first_tile: int
    n_tiles: int
    head_norm: bool
    scale: bool
    outs: tuple


def _inproj_kernel(h_ref, w_ref, wd_ref, col_ref, p_ref, *o_refs, segs, head_dim):
    j = pl.program_id(1)
    h = h_ref[...]
    acc = lax.dot_general(h, w_ref[...], NT_DIMS, preferred_element_type=F32)

    def head_normalised(y):
        pw = p_ref.shape[0]
        parts = []
        for s in range(y.shape[1] // pw):
            a = y[:, s * pw:(s + 1) * pw]
            ss = _dot((a * a).astype(BF16), p_ref[...])
            parts.append(a * lax.rsqrt(ss * (1.0 / head_dim) + EPS))
        return jnp.concatenate(parts, axis=1)

    refs = iter(o_refs)
    for seg in segs:
        seg_refs = [next(refs) for _ in seg.outs]

        @pl.when((j >= seg.first_tile) & (j < seg.first_tile + seg.n_tiles))
        def _(seg=seg, seg_refs=seg_refs):
            y = head_normalised(acc) if seg.head_norm else acc
            if seg.scale:
                y = y * col_ref[...]
            for o_ref, (dtype, channel_major) in zip(seg_refs, seg.outs):
                o_ref[...] = (y.T if channel_major else y).astype(dtype)

    gd_ref = next(refs)

    @pl.when(j == 0)
    def _():
        gd_ref[...] = lax.dot_general(h, wd_ref[...], NT_DIMS, preferred_element_type=F32)


def _inproj_call(rows, dims, h, w_in_t, col, channel_major_kv, tn=512):
    t, d = h.shape
    tm = rows.tm
    sbw = dims.sb_heads * dims.sb_hd
    kw2 = 2 * dims.gla_heads * dims.gla_dk
    vw = dims.gla_heads * dims.gla_dv
    f32_kv = (F32, channel_major_kv)
    layout = [(sbw, True, True, ((BF16, False),)), (sbw, True, True, (f32_kv, (BF16, False))),
              (sbw, False, False, (f32_kv, (BF16, False))), (kw2, False, True, ((F32, False),)),
              (vw, False, False, ((BF16, False),)), (vw, False, False, ((F32, False),))]
    segs, widths, tile = [], [], 0
    for width, head_norm, scale, outs in layout:
        segs.append(_Seg(tile, width // tn, head_norm, scale, outs))
        widths.append(width)
        tile += width // tn
    n_tiles = tile
    gd_block = n_tiles * tn // LANES
    tiles_per_seq = max(rows.seq_len // tm, 1)

    out_specs, out_shapes = [], []
    for seg, width in zip(segs, widths):
        local = lambda j, seg=seg: jnp.clip(j - seg.first_tile, 0, seg.n_tiles - 1)
        for dtype, channel_major in seg.outs:
            if channel_major:
                out_specs.append(pl.BlockSpec((None, tn, tm), lambda i, j, local=local: (
                    i // tiles_per_seq, local(j), i % tiles_per_seq)))
                out_shapes.append(jax.ShapeDtypeStruct((rows.n_seq, width, rows.seq_len), dtype))
            else:
                out_specs.append(pl.BlockSpec((tm, tn), lambda i, j, local=local: (i, local(j))))
                out_shapes.append(jax.ShapeDtypeStruct((t, width), dtype))
    out_specs.append(pl.BlockSpec((tm, LANES), lambda i, j: (i, 0)))
    out_shapes.append(jax.ShapeDtypeStruct((t, LANES), F32))

    pw = 256
    head_of = np.arange(pw) // dims.sb_hd
    return pl.pallas_call(
        functools.partial(_inproj_kernel, segs=tuple(segs), head_dim=dims.sb_hd),
        grid=(t // tm, n_tiles),
        in_specs=[pl.BlockSpec((tm, d), lambda i, j: (i, 0)),
                  pl.BlockSpec((tn, d), lambda i, j: (j, 0)),
                  pl.BlockSpec((LANES, d), lambda i, j: (gd_block, 0)),
                  pl.BlockSpec((1, tn), lambda i, j: (0, j)),
                  pl.BlockSpec((pw, pw), lambda i, j: (0, 0))],
        out_specs=out_specs,
        out_shape=out_shapes,
        compiler_params=pltpu.CompilerParams(dimension_semantics=("arbitrary", "arbitrary"),
                                             vmem_limit_bytes=LARGE_VMEM_LIMIT_BYTES),
        name="inproj",
    )(h, w_in_t, w_in_t, col.reshape(1, -1), jnp.asarray(head_of[:, None] == head_of[None, :], BF16))


def _sb_prompt_kernel(bias_ref, q_ref, k_ref, v_ref, u_ref, o_ref, *, tq, hd, n_pairs):
    group = pl.program_id(1)
    i = pl.program_id(2)
    pw = 2 * hd
    lane = lax.broadcasted_iota(jnp.int32, (tq, pw), 1)
    row = lax.broadcasted_iota(jnp.int32, (2 * tq, tq), 0)
    col = lax.broadcasted_iota(jnp.int32, (2 * tq, tq), 1)
    causal = col < jnp.where(row >= tq, row - tq, row)
    zero = jnp.zeros((tq, pw), q_ref.dtype)
    q2s, biases = [], []
    for p in range(n_pairs):
        q = q_ref[:, p * pw:(p + 1) * pw]
        q2s.append(jnp.concatenate([jnp.where(lane < hd, q, zero), jnp.where(lane >= hd, q, zero)], axis=0))
        head = 2 * (group * n_pairs + p)
        biases.append((bias_ref[head], bias_ref[head + 1]))

    def block(n, carry, masked):
        off = pl.multiple_of((i - n) * tq, tq)
        out = []
        for p, (acc, c) in enumerate(carry):
            lanes = slice(p * pw, (p + 1) * pw)
            z = lax.dot_general(q2s[p], k_ref[pl.ds(off, tq), lanes], NT_DIMS, preferred_element_type=F32)
            z = jnp.concatenate([z[:tq] + biases[p][0], z[tq:] + biases[p][1]], axis=0)
            sp = _softplus2(z)
            if masked:
                sp = jnp.where(causal, sp, 0.0)
            a = jnp.exp2(z + c + _dot(sp.astype(BF16), u_ref[...]))
            if masked:
                a = jnp.where(causal, a, 0.0)
            acc = acc + _dot(a.astype(BF16), v_ref[pl.ds(off, tq), lanes])
            out.append((acc, c - jnp.sum(sp, axis=-1, keepdims=True)))
        return tuple(out)

    init = tuple((jnp.zeros((2 * tq, pw), F32), jnp.zeros((2 * tq, 1), F32)) for _ in range(n_pairs))
    carry = lax.fori_loop(1, i + 1, lambda t, cr: block(t, cr, False), block(0, init, True))
    for p, (acc, _) in enumerate(carry):
        o_ref[:, p * pw:(p + 1) * pw] = jnp.where(lane < hd, acc[:tq], acc[tq:]).astype(o_ref.dtype)


def _sb_prompt_call(q, k, v, bias, n_seq, seq_len, hd, tq=256, n_pairs=8):
    t, width = q.shape
    gw = 2 * hd * n_pairs
    nq = seq_len // tq
    u = jnp.asarray(-np.tril(np.ones((tq, tq), np.float32)), BF16)
    return pl.pallas_call(
        functools.partial(_sb_prompt_kernel, tq=tq, hd=hd, n_pairs=n_pairs),
        grid=(n_seq, width // gw, nq),
        in_specs=[pl.BlockSpec(memory_space=pltpu.SMEM),
                  pl.BlockSpec((tq, gw), lambda b, g, i: (b * nq + i, g)),
                  pl.BlockSpec((seq_len, gw), lambda b, g, i: (b, g)),
                  pl.BlockSpec((seq_len, gw), lambda b, g, i: (b, g)),
                  pl.BlockSpec((tq, tq), lambda b, g, i: (0, 0))],
        out_specs=pl.BlockSpec((tq, gw), lambda b, g, i: (b * nq + i, g)),
        out_shape=jax.ShapeDtypeStruct((t, width), BF16),
        compiler_params=_params("arbitrary", "arbitrary", "arbitrary"),
        name="sb_prompt",
    )(bias, q, k, v, u)


def _sb_sample_kernel(pt_ref, qbd_ref, bias_ref, kn_ref, vn_ref, *rest, n_slots, n_heads, hd, page):
    del pt_ref
    kp_refs = rest[:n_slots]
    vp_refs = rest[n_slots:2 * n_slots]
    u_ref, o_ref, acc_ref, c_ref = rest[2 * n_slots:]
    s = pl.program_id(1)
    qbd = qbd_ref[...]
    n_rows, width = qbd.shape
    n_new = kn_ref.shape[0]

    def block(z, mask, weigh):
        n_keys = z.shape[1]
        gsz = min(n_keys, u_ref.shape[0])
        z = z + bias_ref[...]
        sp = _softplus2(z)
        if mask is not None:
            sp = jnp.where(mask, sp, 0.0)
        c = c_ref[...]
        parts = []
        for g0 in range(0, n_keys, gsz):
            spg = sp[:, g0:g0 + gsz]
            incl = _dot(spg.astype(BF16), u_ref[:gsz, :gsz])
            parts.append(jnp.exp2(z[:, g0:g0 + gsz] + incl + jnp.concatenate([c] * (gsz // c.shape[1]), axis=1)))
            c = c - jnp.sum(spg, axis=-1, keepdims=True)
        a = parts[0] if len(parts) == 1 else jnp.concatenate(parts, axis=1)
        if mask is not None:
            a = jnp.where(mask, a, 0.0)
        acc_ref[...] += weigh(a.astype(BF16))
        c_ref[...] = c

    @pl.when(s == 0)
    def _():
        acc_ref[...] = jnp.zeros_like(acc_ref)
        c_ref[...] = jnp.zeros_like(c_ref)
        pad = jnp.zeros((page - n_new, width), F32)
        kb = jnp.concatenate([kn_ref[...], pad], axis=0).astype(BF16)
        vb = jnp.concatenate([vn_ref[...], pad], axis=0).astype(BF16)
        row = lax.broadcasted_iota(jnp.int32, (n_rows, page), 0)
        col = lax.broadcasted_iota(jnp.int32, (n_rows, page), 1)
        mask = col * n_heads < row - row % n_heads
        block(lax.dot_general(qbd, kb, NT_DIMS, preferred_element_type=F32), mask, lambda a: _dot(a, vb))

    @pl.when(s > 0)
    def _():
        kt = jnp.concatenate([r[...].astype(BF16) for r in kp_refs], axis=1)
        vt = jnp.concatenate([r[...].astype(BF16) for r in vp_refs], axis=1)
        block(_dot(qbd, kt), None, lambda a: lax.dot_general(a, vt, NT_DIMS, preferred_element_type=F32))

    @pl.when(s == pl.num_programs(1) - 1)
    def _():
        row = lax.broadcasted_iota(jnp.int32, (n_rows, width), 0)
        col = lax.broadcasted_iota(jnp.int32, (n_rows, width), 1)
        own = jnp.where(col // hd == row % n_heads, acc_ref[...], 0.0)
        o_ref[...] = jnp.sum(own.reshape(n_rows // n_heads, n_heads, width), axis=1).astype(o_ref.dtype)


def _sb_sample_call(q, k_new, v_new, cache_k, cache_v, page_table, bias, n_heads, hd, n_slots=8):
    bsz, n_q, width = q.shape
    page = cache_k.shape[2]
    n_pages = page_table.shape[1]
    n_steps = n_pages // n_slots
    n_rows = n_q * n_heads
    n_new = 8
    eye = jnp.eye(n_heads, dtype=q.dtype)
    qbd = jnp.einsum("bqhd,hg->bqhgd", q.reshape(bsz, n_q, n_heads, hd), eye).reshape(bsz, n_rows, width)
    bias_rows = jnp.tile(bias.astype(F32), n_q).reshape(n_rows, 1)
    pad = ((0, 0), (0, n_new - n_q), (0, 0))
    k_new, v_new = jnp.pad(k_new, pad), jnp.pad(v_new, pad)
    slot_of, pos_of = np.divmod(np.arange(min(n_slots, 4) * page), page)
    newer_or_same = (slot_of[:, None] < slot_of[None, :]) | (
        (slot_of[:, None] == slot_of[None, :]) & (pos_of[:, None] >= pos_of[None, :]))
    u = jnp.asarray(-newer_or_same.astype(np.float32), BF16)

    def page_spec(slot):
        def imap(b, s, pt):
            return pt[b, n_pages - 1 - (jnp.maximum(s, 1) - 1) * n_slots - slot], 0, 0
        return pl.BlockSpec((None, width, page), imap)

    per_seq = lambda b, s, pt: (b, 0, 0)
    const = lambda b, s, pt: (0, 0)
    grid_spec = pltpu.PrefetchScalarGridSpec(
        num_scalar_prefetch=1,
        grid=(bsz, n_steps + 1),
        in_specs=[pl.BlockSpec((None, n_rows, width), per_seq),
                  pl.BlockSpec((n_rows, 1), const),
                  pl.BlockSpec((None, n_new, width), per_seq),
                  pl.BlockSpec((None, n_new, width), per_seq)]
                 + [page_spec(slot) for slot in range(n_slots)] * 2
                 + [pl.BlockSpec(u.shape, const)],
        out_specs=pl.BlockSpec((None, n_q, width), per_seq),
        scratch_shapes=[pltpu.VMEM((n_rows, width), F32), pltpu.VMEM((n_rows, LANES), F32)],
    )
    return pl.pallas_call(
        functools.partial(_sb_sample_kernel, n_slots=n_slots, n_heads=n_heads, hd=hd, page=page),
        grid_spec=grid_spec,
        out_shape=jax.ShapeDtypeStruct((bsz, n_q, width), BF16),
        compiler_params=_params("arbitrary", "arbitrary"),
        name="sb_sample",
    )(page_table, qbd, bias_rows, k_new, v_new, *([cache_k] * n_slots), *([cache_v] * n_slots), u)


def _gla_tables(c):
    n_lev = int(math.log2(c))
    assert 1 << n_lev == c
    level = np.full((c, c), -1, np.int32)
    sign = np.zeros((n_lev, c, LANES), np.float32)
    idx = np.arange(c)
    for l in range(n_lev):
        half = c >> (l + 1)
        start = idx // (2 * half) * (2 * half)
        bound = start + half
        same = start[:, None] == start[None, :]
        level[same & (idx[:, None] >= bound[:, None]) & (idx[None, :] < bound[None, :])] = l
        sign[l] = np.where(idx >= bound, 1.0, -1.0)[:, None]
    level[idx, idx] = n_lev
    return np.tril(np.ones((c, c), np.float32)), level, sign.reshape(n_lev * c, LANES), n_lev


def _boundary_rows(b, half):
    c, w = b.shape
    sublanes = 8
    if 2 * half >= sublanes:
        return jnp.concatenate([jnp.broadcast_to(b[s + half - 1:s + half], (2 * half, w))
                                for s in range(0, c, 2 * half)], axis=0)
    b3 = b.reshape(c // sublanes, sublanes, w)
    sub = lax.broadcasted_iota(jnp.int32, b3.shape, 1)
    out = None
    for s in range(0, sublanes, 2 * half):
        piece = jnp.broadcast_to(b3[:, s + half - 1:s + half, :], b3.shape)
        out = piece if out is None else jnp.where(sub >= s, piece, out)
    return out.reshape(c, w)


def _pad_rows(x, n):
    if x.shape[0] == n:
        return x
    return jnp.concatenate([x, jnp.zeros((n - x.shape[0],) + x.shape[1:], x.dtype)], axis=0)


def _split3(x):
    hi = x.astype(BF16)
    r = x - hi.astype(F32)
    mid = r.astype(BF16)
    lo = (r - mid.astype(F32)).astype(BF16)
    return hi, mid, lo


def _gla_kernel(q_ref, k_ref, v_ref, gr_ref, gd_ref, wup_ref, bgk_ref, gno_ref, s0_ref, tri_ref, lvl_ref, sgn_ref,
                o_ref, sout_ref, s_ref, *, chunk, n_lev, rank):
    rb = pl.program_id(1)

    @pl.when(rb == 0)
    def _():
        s_ref[...] = s0_ref[...]

    n_rows = q_ref.shape[0]
    n_heads, dk, dv = s_ref.shape
    level = lvl_ref[...]
    tri = tri_ref[...]
    ones = jnp.ones((chunk, dk), BF16)
    for start in range(0, n_rows, chunk):
        valid = min(chunk, n_rows - start)
        rows = slice(start, start + valid)
        z = _dot(_pad_rows(gd_ref[rows, :rank], chunk).astype(BF16), wup_ref[...]) + bgk_ref[...]
        g = _log_sigmoid_neg(-z) * (1.0 / GLA_TAU)
        if valid < chunk:
            g = jnp.where(lax.broadcasted_iota(jnp.int32, g.shape, 0) < valid, g, 0.0)
        g3 = _split3(g)
        b = sum(_dot(tri, gp) for gp in g3)
        q = _pad_rows(q_ref[rows, :], chunk)
        k = _pad_rows(k_ref[rows, :], chunk)
        factored = [(q.astype(BF16), k.astype(BF16))]
        for l in range(n_lev):
            sgn = jnp.concatenate([sgn_ref[l * chunk:(l + 1) * chunk, :]] * (q.shape[1] // LANES), axis=1)
            x = jnp.exp((b - _boundary_rows(b, chunk >> (l + 1))) * sgn)
            factored.append(((q * x).astype(BF16), (k * x).astype(BF16)))
        qb = (q * jnp.exp(b)).astype(BF16)
        kb = (k * jnp.exp(b[chunk - 1:chunk] - b)).astype(BF16)
        for h in range(n_heads):
            kl = slice(h * dk, (h + 1) * dk)
            vl = slice(h * dv, (h + 1) * dv)
            scores = jnp.zeros((chunk, chunk), F32)
            for l, (ql, kf) in enumerate(factored):
                s_l = lax.dot_general(ql[:, kl], kf[:, kl], NT_DIMS, preferred_element_type=F32)
                scores = jnp.where(level == (l - 1 if l else n_lev), s_l, scores)
            v = _pad_rows(v_ref[rows, vl].astype(F32), chunk).astype(BF16)
            state = s_ref[h]
            o = _dot(scores.astype(BF16), v) + _dot(qb[:, kl], state.astype(BF16))
            d_col = jnp.exp(sum(lax.dot_general(gp[:, kl], ones, TN_DIMS, preferred_element_type=F32) for gp in g3))
            s_ref[h] = (state * jnp.concatenate([d_col] * (dv // dk), axis=1)
                        + lax.dot_general(kb[:, kl], v, TN_DIMS, preferred_element_type=F32))
            on = o * lax.rsqrt(jnp.mean(o * o, axis=-1, keepdims=True) + EPS) * gno_ref[...]
            o_ref[rows, vl] = (on[:valid] * _silu(gr_ref[rows, vl])).astype(o_ref.dtype)

    @pl.when(rb == pl.num_programs(1) - 1)
    def _():
        sout_ref[...] = s_ref[...]


def _gla_call(qk, v, gr, gd, w_gk_up, b_gk, gno, s0, chunk, rows_per_step):
    bsz, seq = qk.shape[:2]
    kw = qk.shape[2] // 2
    n_heads, dk, dv = s0.shape[1:]
    rank = w_gk_up.shape[0]
    tri, level, sign, n_lev = _gla_tables(chunk)
    r = rows_per_step
    blk = lambda w: pl.BlockSpec((None, r, w), lambda b, i: (b, i, 0))
    const2 = lambda b, i: (0, 0)
    full = lambda a: pl.BlockSpec(a.shape, const2)
    state_spec = pl.BlockSpec((None, n_heads, dk, dv), lambda b, i: (b, 0, 0, 0))
    return pl.pallas_call(
        functools.partial(_gla_kernel, chunk=chunk, n_lev=n_lev, rank=rank),
        grid=(bsz, seq // r),
        in_specs=[blk(kw), pl.BlockSpec((None, r, kw), lambda b, i: (b, i, 1)),
                  blk(n_heads * dv), blk(n_heads * dv), blk(gd.shape[2]),
                  pl.BlockSpec((rank, kw), const2), pl.BlockSpec((1, kw), const2), pl.BlockSpec((1, dv), const2),
                  state_spec, full(tri), full(level), full(sign)],
        out_specs=[blk(n_heads * dv), state_spec],
        out_shape=[jax.ShapeDtypeStruct((bsz, seq, n_heads * dv), BF16),
                   jax.ShapeDtypeStruct(s0.shape, F32)],
        scratch_shapes=[pltpu.VMEM((n_heads, dk, dv), F32)],
        compiler_params=_params("arbitrary", "arbitrary"),
        name="gla",
    )(qk, qk, v, gr, gd, w_gk_up.astype(BF16), b_gk.reshape(1, -1), gno.reshape(1, dv), s0,
      jnp.asarray(tri, BF16), jnp.asarray(level), jnp.asarray(sign))


def _resident(shape):
    return pl.BlockSpec(shape, lambda *_: (0,) * len(shape), pipeline_mode=pl.Buffered(1))


def _merge_kernel(h_ref, osb_ref, ogla_ref, wmsb_ref, wmgla_ref, wbsb_ref, wbgla_ref, o_ref, *, tn):
    h, o_sb, o_gla = h_ref[...], osb_ref[...], ogla_ref[...]
    for c0 in range(0, o_ref.shape[1], tn):
        cols = slice(c0, c0 + tn)
        gate = lambda w_t_ref: jax.nn.sigmoid(
            lax.dot_general(h, w_t_ref[cols, :], NT_DIMS, preferred_element_type=F32))
        mixed = gate(wmsb_ref) * _dot(o_sb, wbsb_ref[:, cols]) + gate(wmgla_ref) * _dot(o_gla, wbgla_ref[:, cols])
        o_ref[:, cols] = mixed.astype(o_ref.dtype)


def _merge_call(rows, h, o_sb, o_gla, w_msb_t, w_mgla_t, w_br_sb, w_br_gla, tn=512):
    t, d = h.shape
    tm = min(rows.tm, 512)
    row = lambda w: pl.BlockSpec((tm, w), lambda i: (i, 0))
    return pl.pallas_call(
        functools.partial(_merge_kernel, tn=tn),
        grid=(t // tm,),
        in_specs=[row(d), row(o_sb.shape[1]), row(o_gla.shape[1]), _resident(w_msb_t.shape),
                  _resident(w_mgla_t.shape), _resident(w_br_sb.shape), _resident(w_br_gla.shape)],
        out_specs=row(d),
        out_shape=jax.ShapeDtypeStruct((t, d), BF16),
        compiler_params=_params("arbitrary"),
        name="merge",
    )(h, o_sb, o_gla, w_msb_t, w_mgla_t, w_br_sb, w_br_gla)


def _outproj_kernel(mixed_ref, x_ref, w_ref, gt_ref, g_ref, sc_ref, sh_ref, x1_ref, h2_ref):
    x1 = x_ref[...] + gt_ref[...] * _dot(mixed_ref[...], w_ref[...])
    x1_ref[...] = x1
    h2_ref[...] = _rms_modulate(x1, g_ref[...], sc_ref[...], sh_ref[...]).astype(h2_ref.dtype)


def _outproj_call(rows, mixed, x, w_out, gt, g2, sc, sh):
    t, d = x.shape
    rows = rows._replace(tm=min(rows.tm, 512))
    tm = rows.tm
    vecs = [_seq_vec(rows, v, lambda i: i) for v in (gt, sc, sh)]
    row = pl.BlockSpec((tm, d), lambda i: (i, 0))
    return pl.pallas_call(
        _outproj_kernel,
        grid=(t // tm,),
        in_specs=[row, row, _resident(w_out.shape), vecs[0][1],
                  pl.BlockSpec((1, d), lambda i: (0, 0)), vecs[1][1], vecs[2][1]],
        out_specs=[row, row],
        out_shape=[jax.ShapeDtypeStruct((t, d), F32), jax.ShapeDtypeStruct((t, d), BF16)],
        compiler_params=_params("arbitrary"),
        name="outproj",
    )(mixed, x, w_out, vecs[0][0], g2.reshape(1, d), vecs[1][0], vecs[2][0])


CONV_TAIL = 8


def _conv_gate(a, a1, a2, u, cw_ref, cb_ref):
    conv = cb_ref[...] + cw_ref[0:1, :] * a2 + cw_ref[1:2, :] * a1 + cw_ref[2:3, :] * a
    return _silu(conv) * u


def _up_matmuls(h2_ref, wg_ref, wv_ref, w16_ref):
    @pl.when(pl.program_id(1) == 0)
    def _():
        w16_ref[0] = wg_ref[...].astype(BF16)
        w16_ref[1] = wv_ref[...].astype(BF16)

    h2 = h2_ref[...]
    return _dot(h2, w16_ref[0]), _dot(h2, w16_ref[1])


def _up_carry_kernel(h2_ref, wg_ref, wv_ref, cw_ref, cb_ref, g_ref, tail_ref, w16_ref, carry_ref, *, tiles_per_seq):
    i = pl.program_id(1)

    @pl.when(i % tiles_per_seq == 0)
    def _():
        carry_ref[...] = jnp.zeros_like(carry_ref)

    a, u = _up_matmuls(h2_ref, wg_ref, wv_ref, w16_ref)
    tm = a.shape[0]
    prev = carry_ref[...]
    head_row = lax.broadcasted_iota(jnp.int32, prev.shape, 0)
    shifted = []
    for shift in (1, 2):
        r = pltpu.roll(a, shift, 0)
        top = jnp.where(head_row < shift, pltpu.roll(prev, shift, 0), r[:CONV_TAIL])
        shifted.append(jnp.concatenate([top, r[CONV_TAIL:]], axis=0))
    tail = a[tm - CONV_TAIL:]
    carry_ref[...] = tail
    tail_ref[...] = tail
    g_ref[...] = _conv_gate(a, shifted[0], shifted[1], u, cw_ref, cb_ref).astype(g_ref.dtype)


def _up_state_kernel(h2_ref, wg_ref, wv_ref, cw_ref, cb_ref, e1_ref, e2_ref, g_ref, a_ref, w16_ref, *, seq_len):
    a, u = _up_matmuls(h2_ref, wg_ref, wv_ref, w16_ref)
    pos = lax.broadcasted_iota(jnp.int32, a.shape, 0) % seq_len
    a1 = jnp.where(pos < 1, e1_ref[...], pltpu.roll(a, 1, 0))
    a2 = jnp.where(pos < 2, e2_ref[...], pltpu.roll(a, 2, 0))
    a_ref[...] = a
    g_ref[...] = _conv_gate(a, a1, a2, u, cw_ref, cb_ref).astype(g_ref.dtype)


def _up_call(rows, h2, w_up, conv_w, conv_b, conv_state, tn=512):
    t, d = h2.shape
    f = w_up.shape[1] // 2
    n_col = f // tn
    tm = min(rows.tm, 512)
    common_in = [pl.BlockSpec((tm, d), lambda j, i: (i, 0)),
                 pl.BlockSpec((d, tn), lambda j, i: (0, j)),
                 pl.BlockSpec((d, tn), lambda j, i: (0, j + n_col)),
                 pl.BlockSpec(conv_w.shape[:1] + (tn,), lambda j, i: (0, j)),
                 pl.BlockSpec((1, tn), lambda j, i: (0, j))]
    tile = pl.BlockSpec((tm, tn), lambda j, i: (i, j))
    g_shape = jax.ShapeDtypeStruct((t, f), BF16)
    args = (h2, w_up, w_up, conv_w, conv_b.reshape(1, f))
    if rows.seq_len >= tm:
        tiles_per_seq = rows.seq_len // tm
        g, tail = pl.pallas_call(
            functools.partial(_up_carry_kernel, tiles_per_seq=tiles_per_seq),
            grid=(n_col, t // tm),
            in_specs=common_in,
            out_specs=[tile, pl.BlockSpec((None, CONV_TAIL, tn), lambda j, i: (i // tiles_per_seq, 0, j))],
            out_shape=[g_shape, jax.ShapeDtypeStruct((rows.n_seq, CONV_TAIL, f), F32)],
            scratch_shapes=[pltpu.VMEM((2, d, tn), BF16), pltpu.VMEM((CONV_TAIL, tn), F32)],
            compiler_params=_params("arbitrary", "arbitrary"),
            name="up_carry",
        )(*args)
        return g, tail[:, CONV_TAIL - 2:, :]
    n_seq, seq_len = rows.n_seq, rows.seq_len
    zeros = jnp.zeros((n_seq, seq_len - 1, f), F32)
    e1 = jnp.concatenate([conv_state[:, 1:2], zeros], axis=1).reshape(t, f)
    e2 = jnp.concatenate([conv_state[:, 0:2], zeros[:, 1:]], axis=1).reshape(t, f)
    g, a = pl.pallas_call(
        functools.partial(_up_state_kernel, seq_len=seq_len),
        grid=(n_col, t // tm),
        in_specs=common_in + [tile, tile],
        out_specs=[tile, tile],
        out_shape=[g_shape, jax.ShapeDtypeStruct((t, f), F32)],
        scratch_shapes=[pltpu.VMEM((2, d, tn), BF16)],
        compiler_params=_params("arbitrary", "arbitrary"),
        name="up_state",
    )(*args, e1, e2)
    return g, a.reshape(n_seq, seq_len, f)[:, seq_len - 2:, :]


def _down_kernel(g_ref, w_ref, x_ref, gt_ref, o_ref, *, tn):
    g = g_ref[...]
    for c0 in range(0, o_ref.shape[1], tn):
        cols = slice(c0, c0 + tn)
        o_ref[:, cols] = x_ref[:, cols] + gt_ref[:, cols] * _dot(g, w_ref[:, cols])


def _down_call(rows, g, w_down, x1, gt, tn=512):
    t, f = g.shape
    d = w_down.shape[1]
    rows = rows._replace(tm=min(rows.tm, 512))
    tm = rows.tm
    gt_a, gt_s = _seq_vec(rows, gt, lambda i: i)
    row = pl.BlockSpec((tm, d), lambda i: (i, 0))
    return pl.pallas_call(
        functools.partial(_down_kernel, tn=tn),
        grid=(t // tm,),
        in_specs=[pl.BlockSpec((tm, f), lambda i: (i, 0)), _resident(w_down.shape), row, gt_s],
        out_specs=row,
        out_shape=jax.ShapeDtypeStruct((t, d), F32),
        compiler_params=pltpu.CompilerParams(dimension_semantics=("arbitrary",),
                                             vmem_limit_bytes=LARGE_VMEM_LIMIT_BYTES),
        name="down",
    )(g, w_down, x1, gt_a)


class _Weights(NamedTuple):
    norm1_g: jax.Array
    norm2_g: jax.Array
    w_in_t: jax.Array
    w_msb_t: jax.Array
    w_mgla_t: jax.Array
    in_col: jax.Array
    sb_bias: jax.Array
    w_gk_up: jax.Array
    b_gk: jax.Array
    gla_onorm_g: jax.Array
    w_br_sb: jax.Array
    w_br_gla: jax.Array
    w_out: jax.Array
    w_up: jax.Array
    conv_w: jax.Array
    conv_b: jax.Array
    w_down: jax.Array


class _Dims(NamedTuple):
    sb_heads: int
    sb_hd: int
    gla_heads: int
    gla_dk: int
    gla_dv: int
    rank: int


def _prep_weights(dims, norm1_g, norm2_g, w_in, q_norm_g, k_norm_g, sb_bias, w_gk_up, b_gk, gla_onorm_g,
                  w_br_sb, w_br_gla, w_out, w_up, conv_w, conv_b, w_down):
    d = w_in.shape[0]
    sbw = dims.sb_heads * dims.sb_hd
    kw = dims.gla_heads * dims.gla_dk
    vw = dims.gla_heads * dims.gla_dv
    gates_at = 3 * sbw + 2 * kw + 2 * vw + dims.rank
    in_col = jnp.concatenate([
        jnp.tile(q_norm_g, dims.sb_heads) * (dims.sb_hd ** -0.5 * LOG2E), jnp.tile(k_norm_g, dims.sb_heads),
        jnp.ones((sbw,), F32), jnp.full((kw,), dims.gla_dk ** -0.5, F32), jnp.ones((kw + 2 * vw,), F32)])
    w_in_t = w_in.T.astype(BF16)
    return _Weights(
        norm1_g, norm2_g, w_in_t, w_in_t[gates_at:gates_at + d], w_in_t[gates_at + d:gates_at + 2 * d], in_col,
        sb_bias=sb_bias * LOG2E, w_gk_up=w_gk_up, b_gk=b_gk, gla_onorm_g=gla_onorm_g,
        w_br_sb=w_br_sb.astype(BF16), w_br_gla=w_br_gla.astype(BF16), w_out=w_out.astype(BF16),
        w_up=w_up, conv_w=conv_w, conv_b=conv_b, w_down=w_down.astype(BF16))


def _layer_group(dims, w, rows, x, mod, gla_s0, conv_state, paged):
    n_seq, seq_len, d = x.shape
    t = n_seq * seq_len
    x2 = x.reshape(t, d)
    sh1, sc1, gt1, sh2, sc2, gt2 = jnp.split(mod, 6, axis=-1)
    h = _prenorm_call(rows, x2, w.norm1_g, sc1, sh1)

    q, k32, k16, v32, v16, gqk, gv, gr, gd = _inproj_call(rows, dims, h, w.w_in_t, w.in_col,
                                                          channel_major_kv=paged is None)
    sbw = dims.sb_heads * dims.sb_hd
    heads = (n_seq, seq_len, dims.sb_heads, dims.sb_hd)
    if paged is None:
        o_sb = _sb_prompt_call(q, k16, v16, w.sb_bias, n_seq, seq_len, dims.sb_hd)
        k_out, v_out = (jnp.transpose(a.reshape(n_seq, dims.sb_heads, dims.sb_hd, seq_len), (0, 3, 1, 2))
                        for a in (k32, v32))
    else:
        k_out, v_out = k32.reshape(heads), v32.reshape(heads)
        cache_k, cache_v, page_table = paged
        n_pool, page = cache_k.shape[:2]
        channel_major = lambda c: jnp.transpose(c, (0, 2, 3, 1)).reshape(n_pool, sbw, page)
        o_sb = _sb_sample_call(q.reshape(n_seq, seq_len, sbw), k32.reshape(n_seq, seq_len, sbw),
                               v32.reshape(n_seq, seq_len, sbw), channel_major(cache_k), channel_major(cache_v),
                               page_table, w.sb_bias, dims.sb_heads, dims.sb_hd).reshape(t, sbw)

    o_gla, s_new = _gla_call(gqk.reshape(n_seq, seq_len, -1), gv.reshape(n_seq, seq_len, -1),
                             gr.reshape(n_seq, seq_len, -1), gd.reshape(n_seq, seq_len, -1),
                             w.w_gk_up, w.b_gk, w.gla_onorm_g, gla_s0, chunk=128 if seq_len >= 128 else 8,
                             rows_per_step=min(256, seq_len))
    o_gla = o_gla.reshape(t, -1)

    mixed = _merge_call(rows, h, o_sb, o_gla, w.w_msb_t, w.w_mgla_t, w.w_br_sb, w.w_br_gla)
    x1, h2 = _outproj_call(rows, mixed, x2, w.w_out, gt1, w.norm2_g, sc2, sh2)
    g, conv_new = _up_call(rows, h2, w.w_up, w.conv_w, w.conv_b, conv_state)
    y = _down_call(rows, g, w.w_down, x1, gt2)
    return y.reshape(n_seq, seq_len, d), k_out, v_out, s_new, conv_new


def kernel(x_prompt, x_sample, c_prompt, c_sample, cache_k_sb, cache_v_sb, page_table, state_gla, state_ffn_conv, norm1_g, norm2_g, w_mod, b_mod, w_in, q_norm_g, k_norm_g, sb_bias, w_gk_up, b_gk, gla_onorm_g, w_br_sb, w_br_gla, w_out, w_up, conv_w, conv_b, w_down):
    depth = w_in.shape[0]
    bp, lp, d = x_prompt.shape
    bs, ls, _ = x_sample.shape
    dims = _Dims(sb_heads=cache_k_sb.shape[3], sb_hd=cache_k_sb.shape[4], gla_heads=state_gla.shape[2],
                 gla_dk=state_gla.shape[3], gla_dv=state_gla.shape[4], rank=w_gk_up.shape[1])
    rows_p = _Rows(bp, lp, min(1024, lp))
    rows_s = _Rows(bs, ls, bs * ls)
    pad_rows = -(bp + bs) % 8
    c_all = jnp.concatenate([c_prompt, c_sample, jnp.zeros((pad_rows, d), F32)], axis=0)
    xp, xs = x_prompt, x_sample
    outs = [[] for _ in range(8)]
    for l in range(depth):
        w = _prep_weights(dims, norm1_g[l], norm2_g[l], w_in[l], q_norm_g[l], k_norm_g[l], sb_bias[l], w_gk_up[l],
                          b_gk[l], gla_onorm_g[l], w_br_sb[l], w_br_gla[l], w_out[l], w_up[l], conv_w[l],
                          conv_b[l], w_down[l])
        mod = _mod_call(c_all, w_mod[l], b_mod[l])
        zero_state = jnp.zeros((bp,) + state_gla.shape[2:], F32)
        xp, kp, vp, sp, cp = _layer_group(dims, w, rows_p, xp, mod[:bp], zero_state, None, None)
        xs, ks, vs, ss, cs = _layer_group(dims, w, rows_s, xs, mod[bp:bp + bs], state_gla[l], state_ffn_conv[l],
                                          (cache_k_sb[l], cache_v_sb[l], page_table))
        for lst, val in zip(outs, (kp, vp, sp, cp, ks, vs, ss, cs)):
            lst.append(val)
    return (xp, xs) + tuple(jnp.stack(lst) for lst in outs)
```

```python
import functools
import math
from typing import NamedTuple

import jax
import jax.numpy as jnp
import numpy as np
from jax import lax
from jax.experimental import pallas as pl
from jax.experimental.pallas import tpu as pltpu

F32 = jnp.float32
BF16 = jnp.bfloat16
EPS = 1e-6
GLA_TAU = 16.0
LOG2E = 1.4426950408889634
VMEM_LIMIT_BYTES = 48 * 1024 * 1024
LARGE_VMEM_LIMIT_BYTES = 56 * 1024 * 1024
LANES = 128
NT_DIMS = (((1,), (1,)), ((), ()))
TN_DIMS = (((0,), (0,)), ((), ()))


def _params(*sem):
    return pltpu.CompilerParams(dimension_semantics=sem, vmem_limit_bytes=VMEM_LIMIT_BYTES)


def _dot(a, b):
    return jnp.dot(a, b, preferred_element_type=F32)


def _log_sigmoid_neg(z):
    return -(jnp.maximum(z, 0.0) + jnp.log(1.0 + jnp.exp(-jnp.abs(z))))


def _softplus2(z):
    return jnp.maximum(z, 0.0) + jnp.log(1.0 + jnp.exp2(-jnp.abs(z))) * LOG2E


def _silu(x):
    return x * jax.nn.sigmoid(x)


class _Rows(NamedTuple):
    n_seq: int
    seq_len: int
    tm: int

    @property
    def total(self):
        return self.n_seq * self.seq_len

    @property
    def per_row(self):
        return self.seq_len < self.tm


def _seq_vec(rows, v, row_tile_of):
    d = v.shape[1]
    if rows.per_row:
        return jnp.repeat(v, rows.seq_len, axis=0), pl.BlockSpec((rows.tm, d), lambda *g: (row_tile_of(*g), 0))
    tiles_per_seq = rows.seq_len // rows.tm
    return v[:, None, :], pl.BlockSpec((None, 1, d), lambda *g: (row_tile_of(*g) // tiles_per_seq, 0, 0))


def _mod_kernel(c_ref, w_ref, b_ref, o_ref):
    c = c_ref[...]
    o_ref[...] = _dot(_silu(c).astype(BF16), w_ref[...].astype(BF16)) + b_ref[...]


def _mod_call(c, w_mod, b_mod):
    r, d = c.shape
    n = w_mod.shape[1]
    tn = 512
    return pl.pallas_call(
        _mod_kernel,
        grid=(n // tn,),
        in_specs=[pl.BlockSpec((r, d), lambda j: (0, 0)),
                  pl.BlockSpec((d, tn), lambda j: (0, j)),
                  pl.BlockSpec((1, tn), lambda j: (0, j))],
        out_specs=pl.BlockSpec((r, tn), lambda j: (0, j)),
        out_shape=jax.ShapeDtypeStruct((r, n), F32),
        compiler_params=_params("arbitrary"),
        name="mod",
    )(c, w_mod, b_mod.reshape(1, n))


def _rms_modulate(x, g, sc, sh):
    y = x * lax.rsqrt(jnp.mean(x * x, axis=-1, keepdims=True) + EPS) * g
    return y * (1.0 + sc) + sh


def _prenorm_kernel(x_ref, g_ref, sc_ref, sh_ref, o_ref):
    o_ref[...] = _rms_modulate(x_ref[...], g_ref[...], sc_ref[...], sh_ref[...]).astype(o_ref.dtype)


def _prenorm_call(rows, x, g, sc, sh):
    t, d = x.shape
    sc_a, sc_s = _seq_vec(rows, sc, lambda i: i)
    sh_a, sh_s = _seq_vec(rows, sh, lambda i: i)
    return pl.pallas_call(
        _prenorm_kernel,
        grid=(t // rows.tm,),
        in_specs=[pl.BlockSpec((rows.tm, d), lambda i: (i, 0)),
                  pl.BlockSpec((1, d), lambda i: (0, 0)), sc_s, sh_s],
        out_specs=pl.BlockSpec((rows.tm, d), lambda i: (i, 0)),
        out_shape=jax.ShapeDtypeStruct((t, d), BF16),
        compiler_params=_params("arbitrary"),
        name="prenorm",
    )(x, g.reshape(1, d), sc_a, sh_a)


class _Seg(NamedTuple):
    first_tile: int
    n_tiles: int
    head_norm: bool
    scale: bool
    outs: tuple


def _inproj_kernel(h_ref, w_ref, wd_ref, col_ref, p_ref, *o_refs, segs, head_dim):
    j = pl.program_id(1)
    h = h_ref[...]
    acc = lax.dot_general(h, w_ref[...], NT_DIMS, preferred_element_type=F32)

    def head_normalised(y):
        pw = p_ref.shape[0]
        parts = []
        for s in range(y.shape[1] // pw):
            a = y[:, s * pw:(s + 1) * pw]
            ss = _dot((a * a).astype(BF16), p_ref[...])
            parts.append(a * lax.rsqrt(ss * (1.0 / head_dim) + EPS))
        return jnp.concatenate(parts, axis=1)

    refs = iter(o_refs)
    for seg in segs:
        seg_refs = [next(refs) for _ in seg.outs]

        @pl.when((j >= seg.first_tile) & (j < seg.first_tile + seg.n_tiles))
        def _(seg=seg, seg_refs=seg_refs):
            y = head_normalised(acc) if seg.head_norm else acc
            if seg.scale:
                y = y * col_ref[...]
            for o_ref, (dtype, channel_major) in zip(seg_refs, seg.outs):
                o_ref[...] = (y.T if channel_major else y).astype(dtype)

    gd_ref = next(refs)

    @pl.when(j == 0)
    def _():
        gd_ref[...] = lax.dot_general(h, wd_ref[...], NT_DIMS, preferred_element_type=F32)


def _inproj_call(rows, dims, h, w_in_t, col, channel_major_kv, tn=512):
    t, d = h.shape
    tm = rows.tm
    sbw = dims.sb_heads * dims.sb_hd
    kw2 = 2 * dims.gla_heads * dims.gla_dk
    vw = dims.gla_heads * dims.gla_dv
    f32_kv = (F32, channel_major_kv)
    layout = [(sbw, True, True, ((BF16, False),)), (sbw, True, True, (f32_kv, (BF16, False))),
              (sbw, False, False, (f32_kv, (BF16, False))), (kw2, False, True, ((F32, False),)),
              (vw, False, False, ((BF16, False),)), (vw, False, False, ((F32, False),))]
    segs, widths, tile = [], [], 0
    for width, head_norm, scale, outs in layout:
        segs.append(_Seg(tile, width // tn, head_norm, scale, outs))
        widths.append(width)
        tile += width // tn
    n_tiles = tile
    gd_block = n_tiles * tn // LANES
    tiles_per_seq = max(rows.seq_len // tm, 1)

    out_specs, out_shapes = [], []
    for seg, width in zip(segs, widths):
        local = lambda j, seg=seg: jnp.clip(j - seg.first_tile, 0, seg.n_tiles - 1)
        for dtype, channel_major in seg.outs:
            if channel_major:
                out_specs.append(pl.BlockSpec((None, tn, tm), lambda i, j, local=local: (
                    i // tiles_per_seq, local(j), i % tiles_per_seq)))
                out_shapes.append(jax.ShapeDtypeStruct((rows.n_seq, width, rows.seq_len), dtype))
            else:
                out_specs.append(pl.BlockSpec((tm, tn), lambda i, j, local=local: (i, local(j))))
                out_shapes.append(jax.ShapeDtypeStruct((t, width), dtype))
    out_specs.append(pl.BlockSpec((tm, LANES), lambda i, j: (i, 0)))
    out_shapes.append(jax.ShapeDtypeStruct((t, LANES), F32))

    pw = 256
    head_of = np.arange(pw) // dims.sb_hd
    return pl.pallas_call(
        functools.partial(_inproj_kernel, segs=tuple(segs), head_dim=dims.sb_hd),
        grid=(t // tm, n_tiles),
        in_specs=[pl.BlockSpec((tm, d), lambda i, j: (i, 0)),
                  pl.BlockSpec((tn, d), lambda i, j: (j, 0)),
                  pl.BlockSpec((LANES, d), lambda i, j: (gd_block, 0)),
                  pl.BlockSpec((1, tn), lambda i, j: (0, j)),
                  pl.BlockSpec((pw, pw), lambda i, j: (0, 0))],
        out_specs=out_specs,
        out_shape=out_shapes,
        compiler_params=pltpu.CompilerParams(dimension_semantics=("arbitrary", "arbitrary"),
                                             vmem_limit_bytes=LARGE_VMEM_LIMIT_BYTES),
        name="inproj",
    )(h, w_in_t, w_in_t, col.reshape(1, -1), jnp.asarray(head_of[:, None] == head_of[None, :], BF16))


def _sb_prompt_kernel(bias_ref, q_ref, k_ref, v_ref, u_ref, o_ref, *, tq, hd, n_pairs):
    group = pl.program_id(1)
    i = pl.program_id(2)
    pw = 2 * hd
    lane = lax.broadcasted_iota(jnp.int32, (tq, pw), 1)
    row = lax.broadcasted_iota(jnp.int32, (2 * tq, tq), 0)
    col = lax.broadcasted_iota(jnp.int32, (2 * tq, tq), 1)
    causal = col < jnp.where(row >= tq, row - tq, row)
    zero = jnp.zeros((tq, pw), q_ref.dtype)
    q2s, biases = [], []
    for p in range(n_pairs):
        q = q_ref[:, p * pw:(p + 1) * pw]
        q2s.append(jnp.concatenate([jnp.where(lane < hd, q, zero), jnp.where(lane >= hd, q, zero)], axis=0))
        head = 2 * (group * n_pairs + p)
        biases.append((bias_ref[head], bias_ref[head + 1]))

    def block(n, carry, masked):
        off = pl.multiple_of((i - n) * tq, tq)
        out = []
        for p, (acc, c) in enumerate(carry):
            lanes = slice(p * pw, (p + 1) * pw)
            z = lax.dot_general(q2s[p], k_ref[pl.ds(off, tq), lanes], NT_DIMS, preferred_element_type=F32)
            z = jnp.concatenate([z[:tq] + biases[p][0], z[tq:] + biases[p][1]], axis=0)
            sp = _softplus2(z)
            if masked:
                sp = jnp.where(causal, sp, 0.0)
            a = jnp.exp2(z + c + _dot(sp.astype(BF16), u_ref[...]))
            if masked:
                a = jnp.where(causal, a, 0.0)
            acc = acc + _dot(a.astype(BF16), v_ref[pl.ds(off, tq), lanes])
            out.append((acc, c - jnp.sum(sp, axis=-1, keepdims=True)))
        return tuple(out)

    init = tuple((jnp.zeros((2 * tq, pw), F32), jnp.zeros((2 * tq, 1), F32)) for _ in range(n_pairs))
    carry = lax.fori_loop(1, i + 1, lambda t, cr: block(t, cr, False), block(0, init, True))
    for p, (acc, _) in enumerate(carry):
        o_ref[:, p * pw:(p + 1) * pw] = jnp.where(lane < hd, acc[:tq], acc[tq:]).astype(o_ref.dtype)


def _sb_prompt_call(q, k, v, bias, n_seq, seq_len, hd, tq=256, n_pairs=8):
    t, width = q.shape
    gw = 2 * hd * n_pairs
    nq = seq_len // tq
    u = jnp.asarray(-np.tril(np.ones((tq, tq), np.float32)), BF16)
    return pl.pallas_call(
        functools.partial(_sb_prompt_kernel, tq=tq, hd=hd, n_pairs=n_pairs),
        grid=(n_seq, width // gw, nq),
        in_specs=[pl.BlockSpec(memory_space=pltpu.SMEM),
                  pl.BlockSpec((tq, gw), lambda b, g, i: (b * nq + i, g)),
                  pl.BlockSpec((seq_len, gw), lambda b, g, i: (b, g)),
                  pl.BlockSpec((seq_len, gw), lambda b, g, i: (b, g)),
                  pl.BlockSpec((tq, tq), lambda b, g, i: (0, 0))],
        out_specs=pl.BlockSpec((tq, gw), lambda b, g, i: (b * nq + i, g)),
        out_shape=jax.ShapeDtypeStruct((t, width), BF16),
        compiler_params=_params("arbitrary", "arbitrary", "arbitrary"),
        name="sb_prompt",
    )(bias, q, k, v, u)


def _sb_sample_step(s, last, qbd_ref, bias_ref, kn_ref, vn_ref, kp_refs, vp_refs, u_ref, o_ref, acc_ref, c_ref,
                    *, n_heads, hd, page):
    qbd = qbd_ref[...]
    n_rows, width = qbd.shape
    n_new = kn_ref.shape[0]

    def block(z, mask, weigh):
        n_keys = z.shape[1]
        gsz = min(n_keys, u_ref.shape[0])
        z = z + bias_ref[...]
        sp = _softplus2(z)
        if mask is not None:
            sp = jnp.where(mask, sp, 0.0)
        c = c_ref[...]
        parts = []
        for g0 in range(0, n_keys, gsz):
            spg = sp[:, g0:g0 + gsz]
            incl = _dot(spg.astype(BF16), u_ref[:gsz, :gsz])
            parts.append(jnp.exp2(z[:, g0:g0 + gsz] + incl + jnp.concatenate([c] * (gsz // c.shape[1]), axis=1)))
            c = c - jnp.sum(spg, axis=-1, keepdims=True)
        a = parts[0] if len(parts) == 1 else jnp.concatenate(parts, axis=1)
        if mask is not None:
            a = jnp.where(mask, a, 0.0)
        acc_ref[...] += weigh(a.astype(BF16))
        c_ref[...] = c

    @pl.when(s == 0)
    def _():
        acc_ref[...] = jnp.zeros_like(acc_ref)
        c_ref[...] = jnp.zeros_like(c_ref)
        pad = jnp.zeros((page - n_new, width), F32)
        kb = jnp.concatenate([kn_ref[...], pad], axis=0).astype(BF16)
        vb = jnp.concatenate([vn_ref[...], pad], axis=0).astype(BF16)
        row = lax.broadcasted_iota(jnp.int32, (n_rows, page), 0)
        col = lax.broadcasted_iota(jnp.int32, (n_rows, page), 1)
        mask = col * n_heads < row - row % n_heads
        block(lax.dot_general(qbd, kb, NT_DIMS, preferred_element_type=F32), mask, lambda a: _dot(a, vb))

    @pl.when(s > 0)
    def _():
        kt = jnp.concatenate([r[...].astype(BF16) for r in kp_refs], axis=1)
        vt = jnp.concatenate([r[...].astype(BF16) for r in vp_refs], axis=1)
        block(_dot(qbd, kt), None, lambda a: lax.dot_general(a, vt, NT_DIMS, preferred_element_type=F32))

    @pl.when(last)
    def _():
        row = lax.broadcasted_iota(jnp.int32, (n_rows, width), 0)
        col = lax.broadcasted_iota(jnp.int32, (n_rows, width), 1)
        own = jnp.where(col // hd == row % n_heads, acc_ref[...], 0.0)
        o_ref[...] = jnp.sum(own.reshape(n_rows // n_heads, n_heads, width), axis=1).astype(o_ref.dtype)


SB_NEW_ROWS = 8


def _sb_sample_operands(q, k_new, v_new, cache_k, cache_v, page_table, bias, n_heads, hd, n_slots, seq_of, step_of):
    bsz, n_q, width = q.shape
    page = cache_k.shape[2]
    n_pages = page_table.shape[1]
    n_rows = n_q * n_heads
    eye = jnp.eye(n_heads, dtype=q.dtype)
    qbd = jnp.einsum("bqhd,hg->bqhgd", q.reshape(bsz, n_q, n_heads, hd), eye).reshape(bsz, n_rows, width)
    bias_rows = jnp.tile(bias.astype(F32), n_q).reshape(n_rows, 1)
    pad = ((0, 0), (0, SB_NEW_ROWS - n_q), (0, 0))
    k_new, v_new = jnp.pad(k_new, pad), jnp.pad(v_new, pad)
    slot_of, pos_of = np.divmod(np.arange(min(n_slots, 4) * page), page)
    newer_or_same = (slot_of[:, None] < slot_of[None, :]) | (
        (slot_of[:, None] == slot_of[None, :]) & (pos_of[:, None] >= pos_of[None, :]))
    u = jnp.asarray(-newer_or_same.astype(np.float32), BF16)

    def page_spec(slot):
        def imap(*g):
            pt = g[-1]
            first = n_pages - 1 - (jnp.maximum(step_of(*g[:-1]), 1) - 1) * n_slots
            return pt[seq_of(*g[:-1]), first - slot], 0, 0
        return pl.BlockSpec((None, width, page), imap)

    per_seq = lambda *g: (seq_of(*g[:-1]), 0, 0)
    const = lambda *g: (0, 0)
    operands = (qbd, bias_rows, k_new, v_new, *([cache_k] * n_slots), *([cache_v] * n_slots), u)
    in_specs = ([pl.BlockSpec((None, n_rows, width), per_seq), pl.BlockSpec((n_rows, 1), const),
                 pl.BlockSpec((None, SB_NEW_ROWS, width), per_seq), pl.BlockSpec((None, SB_NEW_ROWS, width), per_seq)]
                + [page_spec(slot) for slot in range(n_slots)] * 2 + [pl.BlockSpec(u.shape, const)])
    out_spec = pl.BlockSpec((None, n_q, width), per_seq)
    out_shape = jax.ShapeDtypeStruct((bsz, n_q, width), BF16)
    scratch = [pltpu.VMEM((n_rows, width), F32), pltpu.VMEM((n_rows, LANES), F32)]
    return operands, in_specs, out_spec, out_shape, scratch


def _gla_tables(c):
    n_lev = int(math.log2(c))
    assert 1 << n_lev == c
    level = np.full((c, c), -1, np.int32)
    sign = np.zeros((n_lev, c, LANES), np.float32)
    idx = np.arange(c)
    for l in range(n_lev):
        half = c >> (l + 1)
        start = idx // (2 * half) * (2 * half)
        bound = start + half
        same = start[:, None] == start[None, :]
        level[same & (idx[:, None] >= bound[:, None]) & (idx[None, :] < bound[None, :])] = l
        sign[l] = np.where(idx >= bound, 1.0, -1.0)[:, None]
    level[idx, idx] = n_lev
    return np.tril(np.ones((c, c), np.float32)), level, sign.reshape(n_lev * c, LANES), n_lev


def _boundary_rows(b, half):
    c, w = b.shape
    sublanes = 8
    if 2 * half >= sublanes:
        return jnp.concatenate([jnp.broadcast_to(b[s + half - 1:s + half], (2 * half, w))
                                for s in range(0, c, 2 * half)], axis=0)
    b3 = b.reshape(c // sublanes, sublanes, w)
    sub = lax.broadcasted_iota(jnp.int32, b3.shape, 1)
    out = None
    for s in range(0, sublanes, 2 * half):
        piece = jnp.broadcast_to(b3[:, s + half - 1:s + half, :], b3.shape)
        out = piece if out is None else jnp.where(sub >= s, piece, out)
    return out.reshape(c, w)


def _pad_rows(x, n):
    if x.shape[0] == n:
        return x
    return jnp.concatenate([x, jnp.zeros((n - x.shape[0],) + x.shape[1:], x.dtype)], axis=0)


def _split3(x):
    hi = x.astype(BF16)
    r = x - hi.astype(F32)
    mid = r.astype(BF16)
    lo = (r - mid.astype(F32)).astype(BF16)
    return hi, mid, lo


def _gla_kernel(q_ref, k_ref, v_ref, gr_ref, gd_ref, wup_ref, bgk_ref, gno_ref, s0_ref, tri_ref, lvl_ref, sgn_ref,
                o_ref, sout_ref, s_ref, *, chunk, n_lev, rank):
    rb = pl.program_id(1)

    @pl.when(rb == 0)
    def _():
        s_ref[...] = s0_ref[...]

    n_rows = q_ref.shape[0]
    n_heads, dk, dv = s_ref.shape
    level = lvl_ref[...]
    tri = tri_ref[...]
    ones = jnp.ones((chunk, dk), BF16)
    for start in range(0, n_rows, chunk):
        valid = min(chunk, n_rows - start)
        rows = slice(start, start + valid)
        z = _dot(_pad_rows(gd_ref[rows, :rank], chunk).astype(BF16), wup_ref[...]) + bgk_ref[...]
        g = _log_sigmoid_neg(-z) * (1.0 / GLA_TAU)
        if valid < chunk:
            g = jnp.where(lax.broadcasted_iota(jnp.int32, g.shape, 0) < valid, g, 0.0)
        g3 = _split3(g)
        b = sum(_dot(tri, gp) for gp in g3)
        q = _pad_rows(q_ref[rows, :], chunk)
        k = _pad_rows(k_ref[rows, :], chunk)
        factored = [(q.astype(BF16), k.astype(BF16))]
        for l in range(n_lev):
            sgn = jnp.concatenate([sgn_ref[l * chunk:(l + 1) * chunk, :]] * (q.shape[1] // LANES), axis=1)
            x = jnp.exp((b - _boundary_rows(b, chunk >> (l + 1))) * sgn)
            factored.append(((q * x).astype(BF16), (k * x).astype(BF16)))
        qb = (q * jnp.exp(b)).astype(BF16)
        kb = (k * jnp.exp(b[chunk - 1:chunk] - b)).astype(BF16)
        for h in range(n_heads):
            kl = slice(h * dk, (h + 1) * dk)
            vl = slice(h * dv, (h + 1) * dv)
            scores = jnp.zeros((chunk, chunk), F32)
            for l, (ql, kf) in enumerate(factored):
                s_l = lax.dot_general(ql[:, kl], kf[:, kl], NT_DIMS, preferred_element_type=F32)
                scores = jnp.where(level == (l - 1 if l else n_lev), s_l, scores)
            v = _pad_rows(v_ref[rows, vl].astype(F32), chunk).astype(BF16)
            state = s_ref[h]
            o = _dot(scores.astype(BF16), v) + _dot(qb[:, kl], state.astype(BF16))
            d_col = jnp.exp(sum(lax.dot_general(gp[:, kl], ones, TN_DIMS, preferred_element_type=F32) for gp in g3))
            s_ref[h] = (state * jnp.concatenate([d_col] * (dv // dk), axis=1)
                        + lax.dot_general(kb[:, kl], v, TN_DIMS, preferred_element_type=F32))
            on = o * lax.rsqrt(jnp.mean(o * o, axis=-1, keepdims=True) + EPS) * gno_ref[...]
            o_ref[rows, vl] = (on[:valid] * _silu(gr_ref[rows, vl])).astype(o_ref.dtype)

    @pl.when(rb == pl.num_programs(1) - 1)
    def _():
        sout_ref[...] = s_ref[...]


def _gla_call(qk, v, gr, gd, w_gk_up, b_gk, gno, s0, chunk, rows_per_step):
    bsz, seq = qk.shape[:2]
    kw = qk.shape[2] // 2
    n_heads, dk, dv = s0.shape[1:]
    rank = w_gk_up.shape[0]
    tri, level, sign, n_lev = _gla_tables(chunk)
    r = rows_per_step
    blk = lambda w: pl.BlockSpec((None, r, w), lambda b, i: (b, i, 0))
    const2 = lambda b, i: (0, 0)
    full = lambda a: pl.BlockSpec(a.shape, const2)
    state_spec = pl.BlockSpec((None, n_heads, dk, dv), lambda b, i: (b, 0, 0, 0))
    return pl.pallas_call(
        functools.partial(_gla_kernel, chunk=chunk, n_lev=n_lev, rank=rank),
        grid=(bsz, seq // r),
        in_specs=[blk(kw), pl.BlockSpec((None, r, kw), lambda b, i: (b, i, 1)),
                  blk(n_heads * dv), blk(n_heads * dv), blk(gd.shape[2]),
                  pl.BlockSpec((rank, kw), const2), pl.BlockSpec((1, kw), const2), pl.BlockSpec((1, dv), const2),
                  state_spec, full(tri), full(level), full(sign)],
        out_specs=[blk(n_heads * dv), state_spec],
        out_shape=[jax.ShapeDtypeStruct((bsz, seq, n_heads * dv), BF16),
                   jax.ShapeDtypeStruct(s0.shape, F32)],
        scratch_shapes=[pltpu.VMEM((n_heads, dk, dv), F32)],
        compiler_params=_params("arbitrary", "arbitrary"),
        name="gla",
    )(qk, qk, v, gr, gd, w_gk_up.astype(BF16), b_gk.reshape(1, -1), gno.reshape(1, dv), s0,
      jnp.asarray(tri, BF16), jnp.asarray(level), jnp.asarray(sign))


def _resident(shape):
    return pl.BlockSpec(shape, lambda *_: (0,) * len(shape), pipeline_mode=pl.Buffered(1))


def _merge_kernel(h_ref, osb_ref, ogla_ref, wmsb_ref, wmgla_ref, wbsb_ref, wbgla_ref, o_ref, *, tn):
    h, o_sb, o_gla = h_ref[...], osb_ref[...], ogla_ref[...]
    for c0 in range(0, o_ref.shape[1], tn):
        cols = slice(c0, c0 + tn)
        gate = lambda w_t_ref: jax.nn.sigmoid(
            lax.dot_general(h, w_t_ref[cols, :], NT_DIMS, preferred_element_type=F32))
        mixed = gate(wmsb_ref) * _dot(o_sb, wbsb_ref[:, cols]) + gate(wmgla_ref) * _dot(o_gla, wbgla_ref[:, cols])
        o_ref[:, cols] = mixed.astype(o_ref.dtype)


def _merge_call(rows, h, o_sb, o_gla, w_msb_t, w_mgla_t, w_br_sb, w_br_gla, tn=512):
    t, d = h.shape
    tm = min(rows.tm, 512)
    row = lambda w: pl.BlockSpec((tm, w), lambda i: (i, 0))
    return pl.pallas_call(
        functools.partial(_merge_kernel, tn=tn),
        grid=(t // tm,),
        in_specs=[row(d), row(o_sb.shape[1]), row(o_gla.shape[1]), _resident(w_msb_t.shape),
                  _resident(w_mgla_t.shape), _resident(w_br_sb.shape), _resident(w_br_gla.shape)],
        out_specs=row(d),
        out_shape=jax.ShapeDtypeStruct((t, d), BF16),
        compiler_params=_params("arbitrary"),
        name="merge",
    )(h, o_sb, o_gla, w_msb_t, w_mgla_t, w_br_sb, w_br_gla)


def _outproj_kernel(mixed_ref, x_ref, w_ref, gt_ref, g_ref, sc_ref, sh_ref, x1_ref, h2_ref):
    x1 = x_ref[...] + gt_ref[...] * _dot(mixed_ref[...], w_ref[...])
    x1_ref[...] = x1
    h2_ref[...] = _rms_modulate(x1, g_ref[...], sc_ref[...], sh_ref[...]).astype(h2_ref.dtype)


def _outproj_call(rows, mixed, x, w_out, gt, g2, sc, sh):
    t, d = x.shape
    rows = rows._replace(tm=min(rows.tm, 512))
    tm = rows.tm
    vecs = [_seq_vec(rows, v, lambda i: i) for v in (gt, sc, sh)]
    row = pl.BlockSpec((tm, d), lambda i: (i, 0))
    return pl.pallas_call(
        _outproj_kernel,
        grid=(t // tm,),
        in_specs=[row, row, _resident(w_out.shape), vecs[0][1],
                  pl.BlockSpec((1, d), lambda i: (0, 0)), vecs[1][1], vecs[2][1]],
        out_specs=[row, row],
        out_shape=[jax.ShapeDtypeStruct((t, d), F32), jax.ShapeDtypeStruct((t, d), BF16)],
        compiler_params=_params("arbitrary"),
        name="outproj",
    )(mixed, x, w_out, vecs[0][0], g2.reshape(1, d), vecs[1][0], vecs[2][0])


CONV_TAIL = 8


def _conv_gate(a, a1, a2, u, cw_ref, cb_ref):
    conv = cb_ref[...] + cw_ref[0:1, :] * a2 + cw_ref[1:2, :] * a1 + cw_ref[2:3, :] * a
    return _silu(conv) * u


def _up_matmuls(first_row_tile, h2_ref, wg_ref, wv_ref, w16_ref):
    @pl.when(first_row_tile)
    def _():
        w16_ref[0] = wg_ref[...].astype(BF16)
        w16_ref[1] = wv_ref[...].astype(BF16)

    h2 = h2_ref[...]
    return _dot(h2, w16_ref[0]), _dot(h2, w16_ref[1])


def _up_carry_step(i, h2_ref, wg_ref, wv_ref, cw_ref, cb_ref, g_ref, tail_ref, w16_ref, carry_ref, *, tiles_per_seq):
    @pl.when(i % tiles_per_seq == 0)
    def _():
        carry_ref[...] = jnp.zeros_like(carry_ref)

    a, u = _up_matmuls(i == 0, h2_ref, wg_ref, wv_ref, w16_ref)
    tm = a.shape[0]
    prev = carry_ref[...]
    head_row = lax.broadcasted_iota(jnp.int32, prev.shape, 0)
    shifted = []
    for shift in (1, 2):
        r = pltpu.roll(a, shift, 0)
        top = jnp.where(head_row < shift, pltpu.roll(prev, shift, 0), r[:CONV_TAIL])
        shifted.append(jnp.concatenate([top, r[CONV_TAIL:]], axis=0))
    tail = a[tm - CONV_TAIL:]
    carry_ref[...] = tail
    tail_ref[...] = tail
    g_ref[...] = _conv_gate(a, shifted[0], shifted[1], u, cw_ref, cb_ref).astype(g_ref.dtype)


def _up_state_kernel(h2_ref, wg_ref, wv_ref, cw_ref, cb_ref, e1_ref, e2_ref, g_ref, a_ref, w16_ref, *, seq_len):
    a, u = _up_matmuls(pl.program_id(1) == 0, h2_ref, wg_ref, wv_ref, w16_ref)
    pos = lax.broadcasted_iota(jnp.int32, a.shape, 0) % seq_len
    a1 = jnp.where(pos < 1, e1_ref[...], pltpu.roll(a, 1, 0))
    a2 = jnp.where(pos < 2, e2_ref[...], pltpu.roll(a, 2, 0))
    a_ref[...] = a
    g_ref[...] = _conv_gate(a, a1, a2, u, cw_ref, cb_ref).astype(g_ref.dtype)


def _up_sb_kernel(pt_ref, h2_ref, wg_ref, wv_ref, cw_ref, cb_ref, qbd_ref, bias_ref, kn_ref, vn_ref, *rest,
                  n_slots, up_steps, n_row_tiles, tiles_per_seq, sb_steps, steps_per_seq, n_heads, hd, page):
    del pt_ref
    kp_refs = rest[:n_slots]
    vp_refs = rest[n_slots:2 * n_slots]
    u_ref, g_ref, tail_ref, o_ref, w16_ref, carry_ref, acc_ref, c_ref = rest[2 * n_slots:]
    step = pl.program_id(0)

    @pl.when(step < up_steps)
    def _():
        _up_carry_step(step % n_row_tiles, h2_ref, wg_ref, wv_ref, cw_ref, cb_ref, g_ref, tail_ref, w16_ref,
                       carry_ref, tiles_per_seq=tiles_per_seq)

    @pl.when(step < sb_steps)
    def _():
        s = step % steps_per_seq
        _sb_sample_step(s, s == steps_per_seq - 1, qbd_ref, bias_ref, kn_ref, vn_ref, kp_refs, vp_refs, u_ref,
                        o_ref, acc_ref, c_ref, n_heads=n_heads, hd=hd, page=page)


def _up_sb_call(rows, h2, w_up, conv_w, conv_b, q, k_new, v_new, cache_k, cache_v, page_table, bias, n_heads, hd,
                tm=256, tn=512, n_slots=8):
    t, d = h2.shape
    f = w_up.shape[1] // 2
    n_col = f // tn
    n_row_tiles = t // tm
    assert rows.seq_len % tm == 0
    tiles_per_seq = rows.seq_len // tm
    up_steps = n_col * n_row_tiles
    steps_per_seq = page_table.shape[1] // n_slots + 1
    sb_steps = q.shape[0] * steps_per_seq
    up_at = lambda step: jnp.minimum(step, up_steps - 1)
    sb_at = lambda step: jnp.minimum(step, sb_steps - 1)
    col_of = lambda step: up_at(step) // n_row_tiles
    row_of = lambda step: up_at(step) % n_row_tiles
    sb_ops, sb_in, sb_out, sb_shape, sb_scratch = _sb_sample_operands(
        q, k_new, v_new, cache_k, cache_v, page_table, bias, n_heads, hd, n_slots,
        seq_of=lambda step: sb_at(step) // steps_per_seq, step_of=lambda step: sb_at(step) % steps_per_seq)
    up_in = [pl.BlockSpec((tm, d), lambda step, pt: (row_of(step), 0)),
             pl.BlockSpec((d, tn), lambda step, pt: (0, col_of(step))),
             pl.BlockSpec((d, tn), lambda step, pt: (0, col_of(step) + n_col)),
             pl.BlockSpec(conv_w.shape[:1] + (tn,), lambda step, pt: (0, col_of(step))),
             pl.BlockSpec((1, tn), lambda step, pt: (0, col_of(step)))]
    up_out = [pl.BlockSpec((tm, tn), lambda step, pt: (row_of(step), col_of(step))),
              pl.BlockSpec((None, CONV_TAIL, tn), lambda step, pt: (row_of(step) // tiles_per_seq, 0, col_of(step)))]
    g, tail, o_sb = pl.pallas_call(
        functools.partial(_up_sb_kernel, n_slots=n_slots, up_steps=up_steps, n_row_tiles=n_row_tiles,
                          tiles_per_seq=tiles_per_seq, sb_steps=sb_steps, steps_per_seq=steps_per_seq,
                          n_heads=n_heads, hd=hd, page=cache_k.shape[2]),
        grid_spec=pltpu.PrefetchScalarGridSpec(
            num_scalar_prefetch=1,
            grid=(max(up_steps, sb_steps),),
            in_specs=up_in + sb_in,
            out_specs=up_out + [sb_out],
            scratch_shapes=[pltpu.VMEM((2, d, tn), BF16), pltpu.VMEM((CONV_TAIL, tn), F32)] + sb_scratch),
        out_shape=[jax.ShapeDtypeStruct((t, f), BF16), jax.ShapeDtypeStruct((rows.n_seq, CONV_TAIL, f), F32), sb_shape],
        compiler_params=_params("arbitrary"),
        name="up_sb",
    )(page_table, h2, w_up, w_up, conv_w, conv_b.reshape(1, f), *sb_ops)
    return g, tail[:, CONV_TAIL - 2:, :], o_sb


def _up_state_call(rows, h2, w_up, conv_w, conv_b, conv_state, tn=512):
    t, d = h2.shape
    f = w_up.shape[1] // 2
    n_col = f // tn
    tm = rows.tm
    common_in = [pl.BlockSpec((tm, d), lambda j, i: (i, 0)),
                 pl.BlockSpec((d, tn), lambda j, i: (0, j)),
                 pl.BlockSpec((d, tn), lambda j, i: (0, j + n_col)),
                 pl.BlockSpec(conv_w.shape[:1] + (tn,), lambda j, i: (0, j)),
                 pl.BlockSpec((1, tn), lambda j, i: (0, j))]
    tile = pl.BlockSpec((tm, tn), lambda j, i: (i, j))
    g_shape = jax.ShapeDtypeStruct((t, f), BF16)
    args = (h2, w_up, w_up, conv_w, conv_b.reshape(1, f))
    n_seq, seq_len = rows.n_seq, rows.seq_len
    zeros = jnp.zeros((n_seq, seq_len - 1, f), F32)
    e1 = jnp.concatenate([conv_state[:, 1:2], zeros], axis=1).reshape(t, f)
    e2 = jnp.concatenate([conv_state[:, 0:2], zeros[:, 1:]], axis=1).reshape(t, f)
    g, a = pl.pallas_call(
        functools.partial(_up_state_kernel, seq_len=seq_len),
        grid=(n_col, t // tm),
        in_specs=common_in + [tile, tile],
        out_specs=[tile, tile],
        out_shape=[g_shape, jax.ShapeDtypeStruct((t, f), F32)],
        scratch_shapes=[pltpu.VMEM((2, d, tn), BF16)],
        compiler_params=_params("arbitrary", "arbitrary"),
        name="up_state",
    )(*args, e1, e2)
    return g, a.reshape(n_seq, seq_len, f)[:, seq_len - 2:, :]


def _down_kernel(g_ref, w_ref, x_ref, gt_ref, o_ref, *, tn):
    g = g_ref[...]
    for c0 in range(0, o_ref.shape[1], tn):
        cols = slice(c0, c0 + tn)
        o_ref[:, cols] = x_ref[:, cols] + gt_ref[:, cols] * _dot(g, w_ref[:, cols])


def _down_call(rows, g, w_down, x1, gt, tn=512):
    t, f = g.shape
    d = w_down.shape[1]
    rows = rows._replace(tm=min(rows.tm, 512))
    tm = rows.tm
    gt_a, gt_s = _seq_vec(rows, gt, lambda i: i)
    row = pl.BlockSpec((tm, d), lambda i: (i, 0))
    return pl.pallas_call(
        functools.partial(_down_kernel, tn=tn),
        grid=(t // tm,),
        in_specs=[pl.BlockSpec((tm, f), lambda i: (i, 0)), _resident(w_down.shape), row, gt_s],
        out_specs=row,
        out_shape=jax.ShapeDtypeStruct((t, d), F32),
        compiler_params=pltpu.CompilerParams(dimension_semantics=("arbitrary",),
                                             vmem_limit_bytes=LARGE_VMEM_LIMIT_BYTES),
        name="down",
    )(g, w_down, x1, gt_a)


class _Weights(NamedTuple):
    norm1_g: jax.Array
    norm2_g: jax.Array
    w_in_t: jax.Array
    w_msb_t: jax.Array
    w_mgla_t: jax.Array
    in_col: jax.Array
    sb_bias: jax.Array
    w_gk_up: jax.Array
    b_gk: jax.Array
    gla_onorm_g: jax.Array
    w_br_sb: jax.Array
    w_br_gla: jax.Array
    w_out: jax.Array
    w_up: jax.Array
    conv_w: jax.Array
    conv_b: jax.Array
    w_down: jax.Array


class _Dims(NamedTuple):
    sb_heads: int
    sb_hd: int
    gla_heads: int
    gla_dk: int
    gla_dv: int
    rank: int


def _prep_weights(dims, norm1_g, norm2_g, w_in, q_norm_g, k_norm_g, sb_bias, w_gk_up, b_gk, gla_onorm_g,
                  w_br_sb, w_br_gla, w_out, w_up, conv_w, conv_b, w_down):
    d = w_in.shape[0]
    sbw = dims.sb_heads * dims.sb_hd
    kw = dims.gla_heads * dims.gla_dk
    vw = dims.gla_heads * dims.gla_dv
    gates_at = 3 * sbw + 2 * kw + 2 * vw + dims.rank
    in_col = jnp.concatenate([
        jnp.tile(q_norm_g, dims.sb_heads) * (dims.sb_hd ** -0.5 * LOG2E), jnp.tile(k_norm_g, dims.sb_heads),
        jnp.ones((sbw,), F32), jnp.full((kw,), dims.gla_dk ** -0.5, F32), jnp.ones((kw + 2 * vw,), F32)])
    w_in_t = w_in.T.astype(BF16)
    return _Weights(
        norm1_g, norm2_g, w_in_t, w_in_t[gates_at:gates_at + d], w_in_t[gates_at + d:gates_at + 2 * d], in_col,
        sb_bias=sb_bias * LOG2E, w_gk_up=w_gk_up, b_gk=b_gk, gla_onorm_g=gla_onorm_g,
        w_br_sb=w_br_sb.astype(BF16), w_br_gla=w_br_gla.astype(BF16), w_out=w_out.astype(BF16),
        w_up=w_up, conv_w=conv_w, conv_b=conv_b, w_down=w_down.astype(BF16))


class _Front(NamedTuple):
    x2: jax.Array
    mod: jax.Array
    h: jax.Array
    q: jax.Array
    k32: jax.Array
    k16: jax.Array
    v32: jax.Array
    v16: jax.Array
    gqk: jax.Array
    gv: jax.Array
    gr: jax.Array
    gd: jax.Array


def _front(dims, w, rows, x, mod, channel_major_kv):
    x2 = x.reshape(rows.total, x.shape[2])
    sh1, sc1 = jnp.split(mod, 6, axis=-1)[:2]
    h = _prenorm_call(rows, x2, w.norm1_g, sc1, sh1)
    return _Front(x2, mod, h, *_inproj_call(rows, dims, h, w.w_in_t, w.in_col, channel_major_kv))


def _mixers_to_h2(dims, w, rows, fr, o_sb, gla_s0):
    n_seq, seq_len = rows.n_seq, rows.seq_len
    _, _, gt1, sh2, sc2, _ = jnp.split(fr.mod, 6, axis=-1)
    o_gla, s_new = _gla_call(fr.gqk.reshape(n_seq, seq_len, -1), fr.gv.reshape(n_seq, seq_len, -1),
                             fr.gr.reshape(n_seq, seq_len, -1), fr.gd.reshape(n_seq, seq_len, -1),
                             w.w_gk_up, w.b_gk, w.gla_onorm_g, gla_s0, chunk=128 if seq_len >= 128 else 8,
                             rows_per_step=min(256, seq_len))
    mixed = _merge_call(rows, fr.h, o_sb, o_gla.reshape(rows.total, -1), w.w_msb_t, w.w_mgla_t, w.w_br_sb, w.w_br_gla)
    x1, h2 = _outproj_call(rows, mixed, fr.x2, w.w_out, gt1, w.norm2_g, sc2, sh2)
    return x1, h2, s_new


def _layer(dims, w, rows_p, rows_s, xp, xs, mod_p, mod_s, cache_k, cache_v, page_table, gla_s0, conv_state):
    sbw = dims.sb_heads * dims.sb_hd
    d = xp.shape[2]
    fp = _front(dims, w, rows_p, xp, mod_p, channel_major_kv=True)
    fs = _front(dims, w, rows_s, xs, mod_s, channel_major_kv=False)

    o_sb_p = _sb_prompt_call(fp.q, fp.k16, fp.v16, w.sb_bias, rows_p.n_seq, rows_p.seq_len, dims.sb_hd)
    x1_p, h2_p, s_p = _mixers_to_h2(dims, w, rows_p, fp, o_sb_p,
                                    jnp.zeros((rows_p.n_seq,) + gla_s0.shape[1:], F32))

    n_pool, page = cache_k.shape[:2]
    channel_major = lambda c: jnp.transpose(c, (0, 2, 3, 1)).reshape(n_pool, sbw, page)
    per_seq = lambda a: a.reshape(rows_s.n_seq, rows_s.seq_len, sbw)
    g_p, conv_p, o_sb_s = _up_sb_call(rows_p, h2_p, w.w_up, w.conv_w, w.conv_b, per_seq(fs.q), per_seq(fs.k32),
                                      per_seq(fs.v32), channel_major(cache_k), channel_major(cache_v), page_table,
                                      w.sb_bias, dims.sb_heads, dims.sb_hd)
    yp = _down_call(rows_p, g_p, w.w_down, x1_p, jnp.split(mod_p, 6, axis=-1)[5])

    x1_s, h2_s, s_s = _mixers_to_h2(dims, w, rows_s, fs, o_sb_s.reshape(rows_s.total, sbw), gla_s0)
    g_s, conv_s = _up_state_call(rows_s, h2_s, w.w_up, w.conv_w, w.conv_b, conv_state)
    ys = _down_call(rows_s, g_s, w.w_down, x1_s, jnp.split(mod_s, 6, axis=-1)[5])

    kv_p = [jnp.transpose(a.reshape(rows_p.n_seq, dims.sb_heads, dims.sb_hd, rows_p.seq_len), (0, 3, 1, 2))
            for a in (fp.k32, fp.v32)]
    kv_s = [a.reshape(rows_s.n_seq, rows_s.seq_len, dims.sb_heads, dims.sb_hd) for a in (fs.k32, fs.v32)]
    return (yp.reshape(xp.shape), ys.reshape(xs.shape)), (*kv_p, s_p, conv_p, *kv_s, s_s, conv_s)


def kernel(x_prompt, x_sample, c_prompt, c_sample, cache_k_sb, cache_v_sb, page_table, state_gla, state_ffn_conv, norm1_g, norm2_g, w_mod, b_mod, w_in, q_norm_g, k_norm_g, sb_bias, w_gk_up, b_gk, gla_onorm_g, w_br_sb, w_br_gla, w_out, w_up, conv_w, conv_b, w_down):
    depth = w_in.shape[0]
    bp, lp, d = x_prompt.shape
    bs, ls, _ = x_sample.shape
    dims = _Dims(sb_heads=cache_k_sb.shape[3], sb_hd=cache_k_sb.shape[4], gla_heads=state_gla.shape[2],
                 gla_dk=state_gla.shape[3], gla_dv=state_gla.shape[4], rank=w_gk_up.shape[1])
    rows_p = _Rows(bp, lp, min(1024, lp))
    rows_s = _Rows(bs, ls, bs * ls)
    pad_rows = -(bp + bs) % 8
    c_all = jnp.concatenate([c_prompt, c_sample, jnp.zeros((pad_rows, d), F32)], axis=0)
    xp, xs = x_prompt, x_sample
    outs = [[] for _ in range(8)]
    for l in range(depth):
        w = _prep_weights(dims, norm1_g[l], norm2_g[l], w_in[l], q_norm_g[l], k_norm_g[l], sb_bias[l], w_gk_up[l],
                          b_gk[l], gla_onorm_g[l], w_br_sb[l], w_br_gla[l], w_out[l], w_up[l], conv_w[l],
                          conv_b[l], w_down[l])
        mod = _mod_call(c_all, w_mod[l], b_mod[l])
        (xp, xs), states = _layer(dims, w, rows_p, rows_s, xp, xs, mod[:bp], mod[bp:bp + bs], cache_k_sb[l],
                                  cache_v_sb[l], page_table, state_gla[l], state_ffn_conv[l])
        for lst, val in zip(outs, states):
            lst.append(val)
    return (xp, xs) + tuple(jnp.stack(lst) for lst in outs)
```

```python
import functools
import math
from typing import NamedTuple

import jax
import jax.numpy as jnp
import numpy as np
from jax import lax
from jax.experimental import pallas as pl
from jax.experimental.pallas import tpu as pltpu

F32 = jnp.float32
BF16 = jnp.bfloat16
EPS = 1e-6
GLA_TAU = 16.0
LOG2E = 1.4426950408889634
VMEM_LIMIT_BYTES = 48 * 1024 * 1024
LARGE_VMEM_LIMIT_BYTES = 56 * 1024 * 1024
LANES = 128
NT_DIMS = (((1,), (1,)), ((), ()))
TN_DIMS = (((0,), (0,)), ((), ()))


def _params(*sem):
    return pltpu.CompilerParams(dimension_semantics=sem, vmem_limit_bytes=VMEM_LIMIT_BYTES)


def _dot(a, b):
    return jnp.dot(a, b, preferred_element_type=F32)


def _log_sigmoid_neg(z):
    return -(jnp.maximum(z, 0.0) + jnp.log(1.0 + jnp.exp(-jnp.abs(z))))


def _softplus2(z):
    return jnp.maximum(z, 0.0) + jnp.log(1.0 + jnp.exp2(-jnp.abs(z))) * LOG2E


def _silu(x):
    return x * jax.nn.sigmoid(x)


class _Rows(NamedTuple):
    n_seq: int
    seq_len: int
    tm: int

    @property
    def total(self):
        return self.n_seq * self.seq_len

    @property
    def per_row(self):
        return self.seq_len < self.tm


def _seq_vec(rows, v, row_tile_of):
    d = v.shape[1]
    if rows.per_row:
        return jnp.repeat(v, rows.seq_len, axis=0), pl.BlockSpec((rows.tm, d), lambda *g: (row_tile_of(*g), 0))
    tiles_per_seq = rows.seq_len // rows.tm
    return v[:, None, :], pl.BlockSpec((None, 1, d), lambda *g: (row_tile_of(*g) // tiles_per_seq, 0, 0))


def _mod_kernel(c_ref, w_ref, b_ref, o_ref):
    c = c_ref[...]
    o_ref[...] = _dot(_silu(c).astype(BF16), w_ref[...].astype(BF16)) + b_ref[...]


def _mod_call(c, w_mod, b_mod):
    r, d = c.shape
    n = w_mod.shape[1]
    tn = 512
    return pl.pallas_call(
        _mod_kernel,
        grid=(n // tn,),
        in_specs=[pl.BlockSpec((r, d), lambda j: (0, 0)),
                  pl.BlockSpec((d, tn), lambda j: (0, j)),
                  pl.BlockSpec((1, tn), lambda j: (0, j))],
        out_specs=pl.BlockSpec((r, tn), lambda j: (0, j)),
        out_shape=jax.ShapeDtypeStruct((r, n), F32),
        compiler_params=_params("arbitrary"),
        name="mod",
    )(c, w_mod, b_mod.reshape(1, n))


def _rms_modulate(x, g, sc, sh):
    y = x * lax.rsqrt(jnp.mean(x * x, axis=-1, keepdims=True) + EPS) * g
    return y * (1.0 + sc) + sh


def _prenorm_kernel(x_ref, g_ref, sc_ref, sh_ref, o_ref):
    o_ref[...] = _rms_modulate(x_ref[...], g_ref[...], sc_ref[...], sh_ref[...]).astype(o_ref.dtype)


def _prenorm_call(rows, x, g, sc, sh):
    t, d = x.shape
    sc_a, sc_s = _seq_vec(rows, sc, lambda i: i)
    sh_a, sh_s = _seq_vec(rows, sh, lambda i: i)
    return pl.pallas_call(
        _prenorm_kernel,
        grid=(t // rows.tm,),
        in_specs=[pl.BlockSpec((rows.tm, d), lambda i: (i, 0)),
                  pl.BlockSpec((1, d), lambda i: (0, 0)), sc_s, sh_s],
        out_specs=pl.BlockSpec((rows.tm, d), lambda i: (i, 0)),
        out_shape=jax.ShapeDtypeStruct((t, d), BF16),
        compiler_params=_params("arbitrary"),
        name="prenorm",
    )(x, g.reshape(1, d), sc_a, sh_a)


class _Seg(NamedTuple):
    first_tile: int
    n_tiles: int
    head_norm: bool
    scale: bool
    outs: tuple


def _inproj_kernel(h_ref, w_ref, wd_ref, col_ref, p_ref, *o_refs, segs, head_dim):
    j = pl.program_id(1)
    h = h_ref[...]
    acc = lax.dot_general(h, w_ref[...], NT_DIMS, preferred_element_type=F32)

    def head_normalised(y):
        pw = p_ref.shape[0]
        parts = []
        for s in range(y.shape[1] // pw):
            a = y[:, s * pw:(s + 1) * pw]
            ss = _dot((a * a).astype(BF16), p_ref[...])
            parts.append(a * lax.rsqrt(ss * (1.0 / head_dim) + EPS))
        return jnp.concatenate(parts, axis=1)

    refs = iter(o_refs)
    for seg in segs:
        seg_refs = [next(refs) for _ in seg.outs]

        @pl.when((j >= seg.first_tile) & (j < seg.first_tile + seg.n_tiles))
        def _(seg=seg, seg_refs=seg_refs):
            y = head_normalised(acc) if seg.head_norm else acc
            if seg.scale:
                y = y * col_ref[...]
            for o_ref, (dtype, channel_major) in zip(seg_refs, seg.outs):
                o_ref[...] = (y.T if channel_major else y).astype(dtype)

    gd_ref = next(refs)

    @pl.when(j == 0)
    def _():
        gd_ref[...] = lax.dot_general(h, wd_ref[...], NT_DIMS, preferred_element_type=F32)


def _inproj_call(rows, dims, h, w_in_t, col, channel_major_kv, tn=512):
    t, d = h.shape
    tm = rows.tm
    sbw = dims.sb_heads * dims.sb_hd
    kw2 = 2 * dims.gla_heads * dims.gla_dk
    vw = dims.gla_heads * dims.gla_dv
    f32_kv = (F32, channel_major_kv)
    layout = [(sbw, True, True, ((BF16, False),)), (sbw, True, True, (f32_kv, (BF16, False))),
              (sbw, False, False, (f32_kv, (BF16, False))), (kw2, False, True, ((F32, False),)),
              (vw, False, False, ((BF16, False),)), (vw, False, False, ((F32, False),))]
    segs, widths, tile = [], [], 0
    for width, head_norm, scale, outs in layout:
        segs.append(_Seg(tile, width // tn, head_norm, scale, outs))
        widths.append(width)
        tile += width // tn
    n_tiles = tile
    gd_block = n_tiles * tn // LANES
    tiles_per_seq = max(rows.seq_len // tm, 1)

    out_specs, out_shapes = [], []
    for seg, width in zip(segs, widths):
        local = lambda j, seg=seg: jnp.clip(j - seg.first_tile, 0, seg.n_tiles - 1)
        for dtype, channel_major in seg.outs:
            if channel_major:
                out_specs.append(pl.BlockSpec((None, tn, tm), lambda i, j, local=local: (
                    i // tiles_per_seq, local(j), i % tiles_per_seq)))
                out_shapes.append(jax.ShapeDtypeStruct((rows.n_seq, width, rows.seq_len), dtype))
            else:
                out_specs.append(pl.BlockSpec((tm, tn), lambda i, j, local=local: (i, local(j))))
                out_shapes.append(jax.ShapeDtypeStruct((t, width), dtype))
    out_specs.append(pl.BlockSpec((tm, LANES), lambda i, j: (i, 0)))
    out_shapes.append(jax.ShapeDtypeStruct((t, LANES), F32))

    pw = 256
    head_of = np.arange(pw) // dims.sb_hd
    return pl.pallas_call(
        functools.partial(_inproj_kernel, segs=tuple(segs), head_dim=dims.sb_hd),
        grid=(t // tm, n_tiles),
        in_specs=[pl.BlockSpec((tm, d), lambda i, j: (i, 0)),
                  pl.BlockSpec((tn, d), lambda i, j: (j, 0)),
                  pl.BlockSpec((LANES, d), lambda i, j: (gd_block, 0)),
                  pl.BlockSpec((1, tn), lambda i, j: (0, j)),
                  pl.BlockSpec((pw, pw), lambda i, j: (0, 0))],
        out_specs=out_specs,
        out_shape=out_shapes,
        compiler_params=pltpu.CompilerParams(dimension_semantics=("arbitrary", "arbitrary"),
                                             vmem_limit_bytes=LARGE_VMEM_LIMIT_BYTES),
        name="inproj",
    )(h, w_in_t, w_in_t, col.reshape(1, -1), jnp.asarray(head_of[:, None] == head_of[None, :], BF16))


def _sb_prompt_kernel(bias_ref, q_ref, k_ref, v_ref, u_ref, o_ref, *, tq, hd, n_pairs):
    group = pl.program_id(1)
    i = pl.program_id(2)
    pw = 2 * hd
    lane = lax.broadcasted_iota(jnp.int32, (tq, pw), 1)
    row = lax.broadcasted_iota(jnp.int32, (2 * tq, tq), 0)
    col = lax.broadcasted_iota(jnp.int32, (2 * tq, tq), 1)
    causal = col < jnp.where(row >= tq, row - tq, row)
    zero = jnp.zeros((tq, pw), q_ref.dtype)
    q2s, biases = [], []
    for p in range(n_pairs):
        q = q_ref[:, p * pw:(p + 1) * pw]
        q2s.append(jnp.concatenate([jnp.where(lane < hd, q, zero), jnp.where(lane >= hd, q, zero)], axis=0))
        head = 2 * (group * n_pairs + p)
        biases.append((bias_ref[head], bias_ref[head + 1]))

    def block(n, carry, masked):
        off = pl.multiple_of((i - n) * tq, tq)
        out = []
        for p, (acc, c) in enumerate(carry):
            lanes = slice(p * pw, (p + 1) * pw)
            z = lax.dot_general(q2s[p], k_ref[pl.ds(off, tq), lanes], NT_DIMS, preferred_element_type=F32)
            z = jnp.concatenate([z[:tq] + biases[p][0], z[tq:] + biases[p][1]], axis=0)
            sp = _softplus2(z)
            if masked:
                sp = jnp.where(causal, sp, 0.0)
            a = jnp.exp2(z + c + _dot(sp.astype(BF16), u_ref[...]))
            if masked:
                a = jnp.where(causal, a, 0.0)
            acc = acc + _dot(a.astype(BF16), v_ref[pl.ds(off, tq), lanes])
            out.append((acc, c - jnp.sum(sp, axis=-1, keepdims=True)))
        return tuple(out)

    init = tuple((jnp.zeros((2 * tq, pw), F32), jnp.zeros((2 * tq, 1), F32)) for _ in range(n_pairs))
    carry = lax.fori_loop(1, i + 1, lambda t, cr: block(t, cr, False), block(0, init, True))
    for p, (acc, _) in enumerate(carry):
        o_ref[:, p * pw:(p + 1) * pw] = jnp.where(lane < hd, acc[:tq], acc[tq:]).astype(o_ref.dtype)


def _sb_prompt_call(q, k, v, bias, n_seq, seq_len, hd, tq=256, n_pairs=8):
    t, width = q.shape
    gw = 2 * hd * n_pairs
    nq = seq_len // tq
    u = jnp.asarray(-np.tril(np.ones((tq, tq), np.float32)), BF16)
    return pl.pallas_call(
        functools.partial(_sb_prompt_kernel, tq=tq, hd=hd, n_pairs=n_pairs),
        grid=(n_seq, width // gw, nq),
        in_specs=[pl.BlockSpec(memory_space=pltpu.SMEM),
                  pl.BlockSpec((tq, gw), lambda b, g, i: (b * nq + i, g)),
                  pl.BlockSpec((seq_len, gw), lambda b, g, i: (b, g)),
                  pl.BlockSpec((seq_len, gw), lambda b, g, i: (b, g)),
                  pl.BlockSpec((tq, tq), lambda b, g, i: (0, 0))],
        out_specs=pl.BlockSpec((tq, gw), lambda b, g, i: (b * nq + i, g)),
        out_shape=jax.ShapeDtypeStruct((t, width), BF16),
        compiler_params=_params("arbitrary", "arbitrary", "arbitrary"),
        name="sb_prompt",
    )(bias, q, k, v, u)


def _sb_sample_step(s, last, qbd_ref, bias_ref, kn_ref, vn_ref, kp_refs, vp_refs, u_ref, o_ref, acc_ref, c_ref,
                    *, n_heads, hd, page):
    qbd = qbd_ref[...]
    n_rows, width = qbd.shape
    n_new = kn_ref.shape[0]

    def block(z, mask, weigh):
        n_keys = z.shape[1]
        gsz = min(n_keys, u_ref.shape[0])
        z = z + bias_ref[...]
        sp = _softplus2(z)
        if mask is not None:
            sp = jnp.where(mask, sp, 0.0)
        c = c_ref[...]
        parts = []
        for g0 in range(0, n_keys, gsz):
            spg = sp[:, g0:g0 + gsz]
            incl = _dot(spg.astype(BF16), u_ref[:gsz, :gsz])
            parts.append(jnp.exp2(z[:, g0:g0 + gsz] + incl + jnp.concatenate([c] * (gsz // c.shape[1]), axis=1)))
            c = c - jnp.sum(spg, axis=-1, keepdims=True)
        a = parts[0] if len(parts) == 1 else jnp.concatenate(parts, axis=1)
        if mask is not None:
            a = jnp.where(mask, a, 0.0)
        acc_ref[...] += weigh(a.astype(BF16))
        c_ref[...] = c

    @pl.when(s == 0)
    def _():
        acc_ref[...] = jnp.zeros_like(acc_ref)
        c_ref[...] = jnp.zeros_like(c_ref)
        pad = jnp.zeros((page - n_new, width), F32)
        kb = jnp.concatenate([kn_ref[...], pad], axis=0).astype(BF16)
        vb = jnp.concatenate([vn_ref[...], pad], axis=0).astype(BF16)
        row = lax.broadcasted_iota(jnp.int32, (n_rows, page), 0)
        col = lax.broadcasted_iota(jnp.int32, (n_rows, page), 1)
        mask = col * n_heads < row - row % n_heads
        block(lax.dot_general(qbd, kb, NT_DIMS, preferred_element_type=F32), mask, lambda a: _dot(a, vb))

    @pl.when(s > 0)
    def _():
        kt = jnp.concatenate([r[...].astype(BF16) for r in kp_refs], axis=1)
        vt = jnp.concatenate([r[...].astype(BF16) for r in vp_refs], axis=1)
        block(_dot(qbd, kt), None, lambda a: lax.dot_general(a, vt, NT_DIMS, preferred_element_type=F32))

    @pl.when(last)
    def _():
        row = lax.broadcasted_iota(jnp.int32, (n_rows, width), 0)
        col = lax.broadcasted_iota(jnp.int32, (n_rows, width), 1)
        own = jnp.where(col // hd == row % n_heads, acc_ref[...], 0.0)
        o_ref[...] = jnp.sum(own.reshape(n_rows // n_heads, n_heads, width), axis=1).astype(o_ref.dtype)


SB_NEW_ROWS = 8


def _sb_sample_operands(q, k_new, v_new, cache_k, cache_v, page_table, bias, n_heads, hd, n_slots, seq_of, step_of):
    bsz, n_q, width = q.shape
    page = cache_k.shape[2]
    n_pages = page_table.shape[1]
    n_rows = n_q * n_heads
    eye = jnp.eye(n_heads, dtype=q.dtype)
    qbd = jnp.einsum("bqhd,hg->bqhgd", q.reshape(bsz, n_q, n_heads, hd), eye).reshape(bsz, n_rows, width)
    bias_rows = jnp.tile(bias.astype(F32), n_q).reshape(n_rows, 1)
    pad = ((0, 0), (0, SB_NEW_ROWS - n_q), (0, 0))
    k_new, v_new = jnp.pad(k_new, pad), jnp.pad(v_new, pad)
    slot_of, pos_of = np.divmod(np.arange(min(n_slots, 4) * page), page)
    newer_or_same = (slot_of[:, None] < slot_of[None, :]) | (
        (slot_of[:, None] == slot_of[None, :]) & (pos_of[:, None] >= pos_of[None, :]))
    u = jnp.asarray(-newer_or_same.astype(np.float32), BF16)

    def page_spec(slot):
        def imap(*g):
            pt = g[-1]
            first = n_pages - 1 - (jnp.maximum(step_of(*g[:-1]), 1) - 1) * n_slots
            return pt[seq_of(*g[:-1]), first - slot], 0, 0
        return pl.BlockSpec((None, width, page), imap)

    per_seq = lambda *g: (seq_of(*g[:-1]), 0, 0)
    const = lambda *g: (0, 0)
    operands = (qbd, bias_rows, k_new, v_new, *([cache_k] * n_slots), *([cache_v] * n_slots), u)
    in_specs = ([pl.BlockSpec((None, n_rows, width), per_seq), pl.BlockSpec((n_rows, 1), const),
                 pl.BlockSpec((None, SB_NEW_ROWS, width), per_seq), pl.BlockSpec((None, SB_NEW_ROWS, width), per_seq)]
                + [page_spec(slot) for slot in range(n_slots)] * 2 + [pl.BlockSpec(u.shape, const)])
    out_spec = pl.BlockSpec((None, n_q, width), per_seq)
    out_shape = jax.ShapeDtypeStruct((bsz, n_q, width), BF16)
    scratch = [pltpu.VMEM((n_rows, width), F32), pltpu.VMEM((n_rows, LANES), F32)]
    return operands, in_specs, out_spec, out_shape, scratch


def _gla_tables(c):
    n_lev = int(math.log2(c))
    assert 1 << n_lev == c
    level = np.full((c, c), -1, np.int32)
    sign = np.zeros((n_lev, c, LANES), np.float32)
    idx = np.arange(c)
    for l in range(n_lev):
        half = c >> (l + 1)
        start = idx // (2 * half) * (2 * half)
        bound = start + half
        same = start[:, None] == start[None, :]
        level[same & (idx[:, None] >= bound[:, None]) & (idx[None, :] < bound[None, :])] = l
        sign[l] = np.where(idx >= bound, 1.0, -1.0)[:, None]
    level[idx, idx] = n_lev
    return np.tril(np.ones((c, c), np.float32)), level, sign.reshape(n_lev * c, LANES), n_lev


def _boundary_rows(b, half):
    c, w = b.shape
    sublanes = 8
    if 2 * half >= sublanes:
        return jnp.concatenate([jnp.broadcast_to(b[s + half - 1:s + half], (2 * half, w))
                                for s in range(0, c, 2 * half)], axis=0)
    b3 = b.reshape(c // sublanes, sublanes, w)
    sub = lax.broadcasted_iota(jnp.int32, b3.shape, 1)
    out = None
    for s in range(0, sublanes, 2 * half):
        piece = jnp.broadcast_to(b3[:, s + half - 1:s + half, :], b3.shape)
        out = piece if out is None else jnp.where(sub >= s, piece, out)
    return out.reshape(c, w)


def _pad_rows(x, n):
    if x.shape[0] == n:
        return x
    return jnp.concatenate([x, jnp.zeros((n - x.shape[0],) + x.shape[1:], x.dtype)], axis=0)


def _split3(x):
    hi = x.astype(BF16)
    r = x - hi.astype(F32)
    mid = r.astype(BF16)
    lo = (r - mid.astype(F32)).astype(BF16)
    return hi, mid, lo


def _gla_kernel(q_ref, k_ref, v_ref, gr_ref, gd_ref, wup_ref, bgk_ref, gno_ref, s0_ref, tri_ref, lvl_ref, sgn_ref,
                o_ref, sout_ref, s_ref, *, chunk, n_lev, rank):
    rb = pl.program_id(1)

    @pl.when(rb == 0)
    def _():
        s_ref[...] = s0_ref[...]

    n_rows = q_ref.shape[0]
    n_heads, dk, dv = s_ref.shape
    level = lvl_ref[...]
    tri = tri_ref[...]
    ones = jnp.ones((chunk, dk), BF16)
    for start in range(0, n_rows, chunk):
        valid = min(chunk, n_rows - start)
        rows = slice(start, start + valid)
        z = _dot(_pad_rows(gd_ref[rows, :rank], chunk).astype(BF16), wup_ref[...]) + bgk_ref[...]
        g = _log_sigmoid_neg(-z) * (1.0 / GLA_TAU)
        if valid < chunk:
            g = jnp.where(lax.broadcasted_iota(jnp.int32, g.shape, 0) < valid, g, 0.0)
        g3 = _split3(g)
        b = sum(_dot(tri, gp) for gp in g3)
        q = _pad_rows(q_ref[rows, :], chunk)
        k = _pad_rows(k_ref[rows, :], chunk)
        factored = [(q.astype(BF16), k.astype(BF16))]
        for l in range(n_lev):
            sgn = jnp.concatenate([sgn_ref[l * chunk:(l + 1) * chunk, :]] * (q.shape[1] // LANES), axis=1)
            x = jnp.exp((b - _boundary_rows(b, chunk >> (l + 1))) * sgn)
            factored.append(((q * x).astype(BF16), (k * x).astype(BF16)))
        qb = (q * jnp.exp(b)).astype(BF16)
        kb = (k * jnp.exp(b[chunk - 1:chunk] - b)).astype(BF16)
        for h in range(n_heads):
            kl = slice(h * dk, (h + 1) * dk)
            vl = slice(h * dv, (h + 1) * dv)
            scores = jnp.zeros((chunk, chunk), F32)
            for l, (ql, kf) in enumerate(factored):
                s_l = lax.dot_general(ql[:, kl], kf[:, kl], NT_DIMS, preferred_element_type=F32)
                scores = jnp.where(level == (l - 1 if l else n_lev), s_l, scores)
            v = _pad_rows(v_ref[rows, vl].astype(F32), chunk).astype(BF16)
            state = s_ref[h]
            o = _dot(scores.astype(BF16), v) + _dot(qb[:, kl], state.astype(BF16))
            d_col = jnp.exp(sum(lax.dot_general(gp[:, kl], ones, TN_DIMS, preferred_element_type=F32) for gp in g3))
            s_ref[h] = (state * jnp.concatenate([d_col] * (dv // dk), axis=1)
                        + lax.dot_general(kb[:, kl], v, TN_DIMS, preferred_element_type=F32))
            on = o * lax.rsqrt(jnp.mean(o * o, axis=-1, keepdims=True) + EPS) * gno_ref[...]
            o_ref[rows, vl] = (on[:valid] * _silu(gr_ref[rows, vl])).astype(o_ref.dtype)

    @pl.when(rb == pl.num_programs(1) - 1)
    def _():
        sout_ref[...] = s_ref[...]


def _gla_call(qk, v, gr, gd, w_gk_up, b_gk, gno, s0, chunk, rows_per_step):
    bsz, seq = qk.shape[:2]
    kw = qk.shape[2] // 2
    n_heads, dk, dv = s0.shape[1:]
    rank = w_gk_up.shape[0]
    tri, level, sign, n_lev = _gla_tables(chunk)
    r = rows_per_step
    blk = lambda w: pl.BlockSpec((None, r, w), lambda b, i: (b, i, 0))
    const2 = lambda b, i: (0, 0)
    full = lambda a: pl.BlockSpec(a.shape, const2)
    state_spec = pl.BlockSpec((None, n_heads, dk, dv), lambda b, i: (b, 0, 0, 0))
    return pl.pallas_call(
        functools.partial(_gla_kernel, chunk=chunk, n_lev=n_lev, rank=rank),
        grid=(bsz, seq // r),
        in_specs=[blk(kw), pl.BlockSpec((None, r, kw), lambda b, i: (b, i, 1)),
                  blk(n_heads * dv), blk(n_heads * dv), blk(gd.shape[2]),
                  pl.BlockSpec((rank, kw), const2), pl.BlockSpec((1, kw), const2), pl.BlockSpec((1, dv), const2),
                  state_spec, full(tri), full(level), full(sign)],
        out_specs=[blk(n_heads * dv), state_spec],
        out_shape=[jax.ShapeDtypeStruct((bsz, seq, n_heads * dv), BF16),
                   jax.ShapeDtypeStruct(s0.shape, F32)],
        scratch_shapes=[pltpu.VMEM((n_heads, dk, dv), F32)],
        compiler_params=_params("arbitrary", "arbitrary"),
        name="gla",
    )(qk, qk, v, gr, gd, w_gk_up.astype(BF16), b_gk.reshape(1, -1), gno.reshape(1, dv), s0,
      jnp.asarray(tri, BF16), jnp.asarray(level), jnp.asarray(sign))


def _resident(shape):
    return pl.BlockSpec(shape, lambda *_: (0,) * len(shape), pipeline_mode=pl.Buffered(1))


def _merge_kernel(h_ref, osb_ref, ogla_ref, wmsb_ref, wmgla_ref, wbsb_ref, wbgla_ref, o_ref, *, tn):
    h, o_sb, o_gla = h_ref[...], osb_ref[...], ogla_ref[...]
    for c0 in range(0, o_ref.shape[1], tn):
        cols = slice(c0, c0 + tn)
        gate = lambda w_t_ref: jax.nn.sigmoid(
            lax.dot_general(h, w_t_ref[cols, :], NT_DIMS, preferred_element_type=F32))
        mixed = gate(wmsb_ref) * _dot(o_sb, wbsb_ref[:, cols]) + gate(wmgla_ref) * _dot(o_gla, wbgla_ref[:, cols])
        o_ref[:, cols] = mixed.astype(o_ref.dtype)


def _merge_call(rows, h, o_sb, o_gla, w_msb_t, w_mgla_t, w_br_sb, w_br_gla, tn=512):
    t, d = h.shape
    tm = min(rows.tm, 512)
    row = lambda w: pl.BlockSpec((tm, w), lambda i: (i, 0))
    return pl.pallas_call(
        functools.partial(_merge_kernel, tn=tn),
        grid=(t // tm,),
        in_specs=[row(d), row(o_sb.shape[1]), row(o_gla.shape[1]), _resident(w_msb_t.shape),
                  _resident(w_mgla_t.shape), _resident(w_br_sb.shape), _resident(w_br_gla.shape)],
        out_specs=row(d),
        out_shape=jax.ShapeDtypeStruct((t, d), BF16),
        compiler_params=_params("arbitrary"),
        name="merge",
    )(h, o_sb, o_gla, w_msb_t, w_mgla_t, w_br_sb, w_br_gla)


def _outproj_kernel(mixed_ref, x_ref, w_ref, gt_ref, g_ref, sc_ref, sh_ref, x1_ref, h2_ref):
    x1 = x_ref[...] + gt_ref[...] * _dot(mixed_ref[...], w_ref[...])
    x1_ref[...] = x1
    h2_ref[...] = _rms_modulate(x1, g_ref[...], sc_ref[...], sh_ref[...]).astype(h2_ref.dtype)


def _outproj_call(rows, mixed, x, w_out, gt, g2, sc, sh):
    t, d = x.shape
    rows = rows._replace(tm=min(rows.tm, 512))
    tm = rows.tm
    vecs = [_seq_vec(rows, v, lambda i: i) for v in (gt, sc, sh)]
    row = pl.BlockSpec((tm, d), lambda i: (i, 0))
    return pl.pallas_call(
        _outproj_kernel,
        grid=(t // tm,),
        in_specs=[row, row, _resident(w_out.shape), vecs[0][1],
                  pl.BlockSpec((1, d), lambda i: (0, 0)), vecs[1][1], vecs[2][1]],
        out_specs=[row, row],
        out_shape=[jax.ShapeDtypeStruct((t, d), F32), jax.ShapeDtypeStruct((t, d), BF16)],
        compiler_params=_params("arbitrary"),
        name="outproj",
    )(mixed, x, w_out, vecs[0][0], g2.reshape(1, d), vecs[1][0], vecs[2][0])


CONV_TAIL = 8


def _conv_gate(a, a1, a2, u, cw_ref, cb_ref):
    conv = cb_ref[...] + cw_ref[0:1, :] * a2 + cw_ref[1:2, :] * a1 + cw_ref[2:3, :] * a
    return _silu(conv) * u


def _up_matmuls(first_row_tile, h2_ref, wg_ref, wv_ref, w16_ref):
    @pl.when(first_row_tile)
    def _():
        w16_ref[0] = wg_ref[...].astype(BF16)
        w16_ref[1] = wv_ref[...].astype(BF16)

    h2 = h2_ref[...]
    return _dot(h2, w16_ref[0]), _dot(h2, w16_ref[1])


def _up_carry_step(i, h2_ref, wg_ref, wv_ref, cw_ref, cb_ref, g_ref, tail_ref, w16_ref, carry_ref, *, tiles_per_seq):
    @pl.when(i % tiles_per_seq == 0)
    def _():
        carry_ref[...] = jnp.zeros_like(carry_ref)

    a, u = _up_matmuls(i == 0, h2_ref, wg_ref, wv_ref, w16_ref)
    tm = a.shape[0]
    prev = carry_ref[...]
    head_row = lax.broadcasted_iota(jnp.int32, prev.shape, 0)
    shifted = []
    for shift in (1, 2):
        r = pltpu.roll(a, shift, 0)
        top = jnp.where(head_row < shift, pltpu.roll(prev, shift, 0), r[:CONV_TAIL])
        shifted.append(jnp.concatenate([top, r[CONV_TAIL:]], axis=0))
    tail = a[tm - CONV_TAIL:]
    carry_ref[...] = tail
    tail_ref[...] = tail
    g_ref[...] = _conv_gate(a, shifted[0], shifted[1], u, cw_ref, cb_ref).astype(g_ref.dtype)


def _up_state_kernel(h2_ref, wg_ref, wv_ref, cw_ref, cb_ref, e1_ref, e2_ref, g_ref, a_ref, w16_ref, *, seq_len):
    a, u = _up_matmuls(pl.program_id(1) == 0, h2_ref, wg_ref, wv_ref, w16_ref)
    pos = lax.broadcasted_iota(jnp.int32, a.shape, 0) % seq_len
    a1 = jnp.where(pos < 1, e1_ref[...], pltpu.roll(a, 1, 0))
    a2 = jnp.where(pos < 2, e2_ref[...], pltpu.roll(a, 2, 0))
    a_ref[...] = a
    g_ref[...] = _conv_gate(a, a1, a2, u, cw_ref, cb_ref).astype(g_ref.dtype)


def _up_sb_kernel(pt_ref, h2_ref, wg_ref, wv_ref, cw_ref, cb_ref, qbd_ref, bias_ref, kn_ref, vn_ref, *rest,
                  n_slots, up_steps, n_row_tiles, tiles_per_seq, sb_steps, steps_per_seq, n_heads, hd, page):
    del pt_ref
    kp_refs = rest[:n_slots]
    vp_refs = rest[n_slots:2 * n_slots]
    u_ref, g_ref, tail_ref, o_ref, w16_ref, carry_ref, acc_ref, c_ref = rest[2 * n_slots:]
    step = pl.program_id(0)

    @pl.when(step < up_steps)
    def _():
        _up_carry_step(step % n_row_tiles, h2_ref, wg_ref, wv_ref, cw_ref, cb_ref, g_ref, tail_ref, w16_ref,
                       carry_ref, tiles_per_seq=tiles_per_seq)

    @pl.when(step < sb_steps)
    def _():
        s = step % steps_per_seq
        _sb_sample_step(s, s == steps_per_seq - 1, qbd_ref, bias_ref, kn_ref, vn_ref, kp_refs, vp_refs, u_ref,
                        o_ref, acc_ref, c_ref, n_heads=n_heads, hd=hd, page=page)


def _up_sb_call(rows, h2, w_up, conv_w, conv_b, q, k_new, v_new, cache_k, cache_v, page_table, bias, n_heads, hd,
                tm=512, tn=512, n_slots=8):
    t, d = h2.shape
    f = w_up.shape[1] // 2
    n_col = f // tn
    n_row_tiles = t // tm
    assert rows.seq_len % tm == 0
    tiles_per_seq = rows.seq_len // tm
    up_steps = n_col * n_row_tiles
    steps_per_seq = page_table.shape[1] // n_slots + 1
    sb_steps = q.shape[0] * steps_per_seq
    up_at = lambda step: jnp.minimum(step, up_steps - 1)
    sb_at = lambda step: jnp.minimum(step, sb_steps - 1)
    col_of = lambda step: up_at(step) // n_row_tiles
    row_of = lambda step: up_at(step) % n_row_tiles
    sb_ops, sb_in, sb_out, sb_shape, sb_scratch = _sb_sample_operands(
        q, k_new, v_new, cache_k, cache_v, page_table, bias, n_heads, hd, n_slots,
        seq_of=lambda step: sb_at(step) // steps_per_seq, step_of=lambda step: sb_at(step) % steps_per_seq)
    up_in = [pl.BlockSpec((tm, d), lambda step, pt: (row_of(step), 0)),
             pl.BlockSpec((d, tn), lambda step, pt: (0, col_of(step))),
             pl.BlockSpec((d, tn), lambda step, pt: (0, col_of(step) + n_col)),
             pl.BlockSpec(conv_w.shape[:1] + (tn,), lambda step, pt: (0, col_of(step))),
             pl.BlockSpec((1, tn), lambda step, pt: (0, col_of(step)))]
    up_out = [pl.BlockSpec((tm, tn), lambda step, pt: (row_of(step), col_of(step))),
              pl.BlockSpec((None, CONV_TAIL, tn), lambda step, pt: (row_of(step) // tiles_per_seq, 0, col_of(step)))]
    g, tail, o_sb = pl.pallas_call(
        functools.partial(_up_sb_kernel, n_slots=n_slots, up_steps=up_steps, n_row_tiles=n_row_tiles,
                          tiles_per_seq=tiles_per_seq, sb_steps=sb_steps, steps_per_seq=steps_per_seq,
                          n_heads=n_heads, hd=hd, page=cache_k.shape[2]),
        grid_spec=pltpu.PrefetchScalarGridSpec(
            num_scalar_prefetch=1,
            grid=(max(up_steps, sb_steps),),
            in_specs=up_in + sb_in,
            out_specs=up_out + [sb_out],
            scratch_shapes=[pltpu.VMEM((2, d, tn), BF16), pltpu.VMEM((CONV_TAIL, tn), F32)] + sb_scratch),
        out_shape=[jax.ShapeDtypeStruct((t, f), BF16), jax.ShapeDtypeStruct((rows.n_seq, CONV_TAIL, f), F32), sb_shape],
        compiler_params=_params("arbitrary"),
        name="up_sb",
    )(page_table, h2, w_up, w_up, conv_w, conv_b.reshape(1, f), *sb_ops)
    return g, tail[:, CONV_TAIL - 2:, :], o_sb


def _up_state_call(rows, h2, w_up, conv_w, conv_b, conv_state, tn=512):
    t, d = h2.shape
    f = w_up.shape[1] // 2
    n_col = f // tn
    tm = rows.tm
    common_in = [pl.BlockSpec((tm, d), lambda j, i: (i, 0)),
                 pl.BlockSpec((d, tn), lambda j, i: (0, j)),
                 pl.BlockSpec((d, tn), lambda j, i: (0, j + n_col)),
                 pl.BlockSpec(conv_w.shape[:1] + (tn,), lambda j, i: (0, j)),
                 pl.BlockSpec((1, tn), lambda j, i: (0, j))]
    tile = pl.BlockSpec((tm, tn), lambda j, i: (i, j))
    g_shape = jax.ShapeDtypeStruct((t, f), BF16)
    args = (h2, w_up, w_up, conv_w, conv_b.reshape(1, f))
    n_seq, seq_len = rows.n_seq, rows.seq_len
    zeros = jnp.zeros((n_seq, seq_len - 1, f), F32)
    e1 = jnp.concatenate([conv_state[:, 1:2], zeros], axis=1).reshape(t, f)
    e2 = jnp.concatenate([conv_state[:, 0:2], zeros[:, 1:]], axis=1).reshape(t, f)
    g, a = pl.pallas_call(
        functools.partial(_up_state_kernel, seq_len=seq_len),
        grid=(n_col, t // tm),
        in_specs=common_in + [tile, tile],
        out_specs=[tile, tile],
        out_shape=[g_shape, jax.ShapeDtypeStruct((t, f), F32)],
        scratch_shapes=[pltpu.VMEM((2, d, tn), BF16)],
        compiler_params=_params("arbitrary", "arbitrary"),
        name="up_state",
    )(*args, e1, e2)
    return g, a.reshape(n_seq, seq_len, f)[:, seq_len - 2:, :]


def _down_kernel(g_ref, w_ref, x_ref, gt_ref, o_ref, *, tn):
    g = g_ref[...]
    for c0 in range(0, o_ref.shape[1], tn):
        cols = slice(c0, c0 + tn)
        o_ref[:, cols] = x_ref[:, cols] + gt_ref[:, cols] * _dot(g, w_ref[:, cols])


def _down_call(rows, g, w_down, x1, gt, tn=512):
    t, f = g.shape
    d = w_down.shape[1]
    rows = rows._replace(tm=min(rows.tm, 512))
    tm = rows.tm
    gt_a, gt_s = _seq_vec(rows, gt, lambda i: i)
    row = pl.BlockSpec((tm, d), lambda i: (i, 0))
    return pl.pallas_call(
        functools.partial(_down_kernel, tn=tn),
        grid=(t // tm,),
        in_specs=[pl.BlockSpec((tm, f), lambda i: (i, 0)), _resident(w_down.shape), row, gt_s],
        out_specs=row,
        out_shape=jax.ShapeDtypeStruct((t, d), F32),
        compiler_params=pltpu.CompilerParams(dimension_semantics=("arbitrary",),
                                             vmem_limit_bytes=LARGE_VMEM_LIMIT_BYTES),
        name="down",
    )(g, w_down, x1, gt_a)


class _Weights(NamedTuple):
    norm1_g: jax.Array
    norm2_g: jax.Array
    w_in_t: jax.Array
    w_msb_t: jax.Array
    w_mgla_t: jax.Array
    in_col: jax.Array
    sb_bias: jax.Array
    w_gk_up: jax.Array
    b_gk: jax.Array
    gla_onorm_g: jax.Array
    w_br_sb: jax.Array
    w_br_gla: jax.Array
    w_out: jax.Array
    w_up: jax.Array
    conv_w: jax.Array
    conv_b: jax.Array
    w_down: jax.Array


class _Dims(NamedTuple):
    sb_heads: int
    sb_hd: int
    gla_heads: int
    gla_dk: int
    gla_dv: int
    rank: int


def _prep_weights(dims, norm1_g, norm2_g, w_in, q_norm_g, k_norm_g, sb_bias, w_gk_up, b_gk, gla_onorm_g,
                  w_br_sb, w_br_gla, w_out, w_up, conv_w, conv_b, w_down):
    d = w_in.shape[0]
    sbw = dims.sb_heads * dims.sb_hd
    kw = dims.gla_heads * dims.gla_dk
    vw = dims.gla_heads * dims.gla_dv
    gates_at = 3 * sbw + 2 * kw + 2 * vw + dims.rank
    in_col = jnp.concatenate([
        jnp.tile(q_norm_g, dims.sb_heads) * (dims.sb_hd ** -0.5 * LOG2E), jnp.tile(k_norm_g, dims.sb_heads),
        jnp.ones((sbw,), F32), jnp.full((kw,), dims.gla_dk ** -0.5, F32), jnp.ones((kw + 2 * vw,), F32)])
    w_in_t = w_in.T.astype(BF16)
    return _Weights(
        norm1_g, norm2_g, w_in_t, w_in_t[gates_at:gates_at + d], w_in_t[gates_at + d:gates_at + 2 * d], in_col,
        sb_bias=sb_bias * LOG2E, w_gk_up=w_gk_up, b_gk=b_gk, gla_onorm_g=gla_onorm_g,
        w_br_sb=w_br_sb.astype(BF16), w_br_gla=w_br_gla.astype(BF16), w_out=w_out.astype(BF16),
        w_up=w_up, conv_w=conv_w, conv_b=conv_b, w_down=w_down.astype(BF16))


class _Front(NamedTuple):
    x2: jax.Array
    mod: jax.Array
    h: jax.Array
    q: jax.Array
    k32: jax.Array
    k16: jax.Array
    v32: jax.Array
    v16: jax.Array
    gqk: jax.Array
    gv: jax.Array
    gr: jax.Array
    gd: jax.Array


def _front(dims, w, rows, x, mod, channel_major_kv):
    x2 = x.reshape(rows.total, x.shape[2])
    sh1, sc1 = jnp.split(mod, 6, axis=-1)[:2]
    h = _prenorm_call(rows, x2, w.norm1_g, sc1, sh1)
    return _Front(x2, mod, h, *_inproj_call(rows, dims, h, w.w_in_t, w.in_col, channel_major_kv))


def _mixers_to_h2(dims, w, rows, fr, o_sb, gla_s0):
    n_seq, seq_len = rows.n_seq, rows.seq_len
    _, _, gt1, sh2, sc2, _ = jnp.split(fr.mod, 6, axis=-1)
    o_gla, s_new = _gla_call(fr.gqk.reshape(n_seq, seq_len, -1), fr.gv.reshape(n_seq, seq_len, -1),
                             fr.gr.reshape(n_seq, seq_len, -1), fr.gd.reshape(n_seq, seq_len, -1),
                             w.w_gk_up, w.b_gk, w.gla_onorm_g, gla_s0, chunk=128 if seq_len >= 128 else 8,
                             rows_per_step=min(256, seq_len))
    mixed = _merge_call(rows, fr.h, o_sb, o_gla.reshape(rows.total, -1), w.w_msb_t, w.w_mgla_t, w.w_br_sb, w.w_br_gla)
    x1, h2 = _outproj_call(rows, mixed, fr.x2, w.w_out, gt1, w.norm2_g, sc2, sh2)
    return x1, h2, s_new


def _layer(dims, w, rows_p, rows_s, xp, xs, mod_p, mod_s, cache_k, cache_v, page_table, gla_s0, conv_state):
    sbw = dims.sb_heads * dims.sb_hd
    d = xp.shape[2]
    fp = _front(dims, w, rows_p, xp, mod_p, channel_major_kv=True)
    fs = _front(dims, w, rows_s, xs, mod_s, channel_major_kv=False)

    o_sb_p = _sb_prompt_call(fp.q, fp.k16, fp.v16, w.sb_bias, rows_p.n_seq, rows_p.seq_len, dims.sb_hd)
    x1_p, h2_p, s_p = _mixers_to_h2(dims, w, rows_p, fp, o_sb_p,
                                    jnp.zeros((rows_p.n_seq,) + gla_s0.shape[1:], F32))

    n_pool, page = cache_k.shape[:2]
    channel_major = lambda c: jnp.transpose(c, (0, 2, 3, 1)).reshape(n_pool, sbw, page)
    per_seq = lambda a: a.reshape(rows_s.n_seq, rows_s.seq_len, sbw)
    g_p, conv_p, o_sb_s = _up_sb_call(rows_p, h2_p, w.w_up, w.conv_w, w.conv_b, per_seq(fs.q), per_seq(fs.k32),
                                      per_seq(fs.v32), channel_major(cache_k), channel_major(cache_v), page_table,
                                      w.sb_bias, dims.sb_heads, dims.sb_hd)
    yp = _down_call(rows_p, g_p, w.w_down, x1_p, jnp.split(mod_p, 6, axis=-1)[5])

    x1_s, h2_s, s_s = _mixers_to_h2(dims, w, rows_s, fs, o_sb_s.reshape(rows_s.total, sbw), gla_s0)
    g_s, conv_s = _up_state_call(rows_s, h2_s, w.w_up, w.conv_w, w.conv_b, conv_state)
    ys = _down_call(rows_s, g_s, w.w_down, x1_s, jnp.split(mod_s, 6, axis=-1)[5])

    kv_p = [jnp.transpose(a.reshape(rows_p.n_seq, dims.sb_heads, dims.sb_hd, rows_p.seq_len), (0, 3, 1, 2))
            for a in (fp.k32, fp.v32)]
    kv_s = [a.reshape(rows_s.n_seq, rows_s.seq_len, dims.sb_heads, dims.sb_hd) for a in (fs.k32, fs.v32)]
    return (yp.reshape(xp.shape), ys.reshape(xs.shape)), (*kv_p, s_p, conv_p, *kv_s, s_s, conv_s)


def kernel(x_prompt, x_sample, c_prompt, c_sample, cache_k_sb, cache_v_sb, page_table, state_gla, state_ffn_conv, norm1_g, norm2_g, w_mod, b_mod, w_in, q_norm_g, k_norm_g, sb_bias, w_gk_up, b_gk, gla_onorm_g, w_br_sb, w_br_gla, w_out, w_up, conv_w, conv_b, w_down):
    depth = w_in.shape[0]
    bp, lp, d = x_prompt.shape
    bs, ls, _ = x_sample.shape
    dims = _Dims(sb_heads=cache_k_sb.shape[3], sb_hd=cache_k_sb.shape[4], gla_heads=state_gla.shape[2],
                 gla_dk=state_gla.shape[3], gla_dv=state_gla.shape[4], rank=w_gk_up.shape[1])
    rows_p = _Rows(bp, lp, min(1024, lp))
    rows_s = _Rows(bs, ls, bs * ls)
    pad_rows = -(bp + bs) % 8
    c_all = jnp.concatenate([c_prompt, c_sample, jnp.zeros((pad_rows, d), F32)], axis=0)
    xp, xs = x_prompt, x_sample
    outs = [[] for _ in range(8)]
    for l in range(depth):
        w = _prep_weights(dims, norm1_g[l], norm2_g[l], w_in[l], q_norm_g[l], k_norm_g[l], sb_bias[l], w_gk_up[l],
                          b_gk[l], gla_onorm_g[l], w_br_sb[l], w_br_gla[l], w_out[l], w_up[l], conv_w[l],
                          conv_b[l], w_down[l])
        mod = _mod_call(c_all, w_mod[l], b_mod[l])
        (xp, xs), states = _layer(dims, w, rows_p, rows_s, xp, xs, mod[:bp], mod[bp:bp + bs], cache_k_sb[l],
                                  cache_v_sb[l], page_table, state_gla[l], state_ffn_conv[l])
        for lst, val in zip(outs, states):
            lst.append(val)
    return (xp, xs) + tuple(jnp.stack(lst) for lst in outs)
```
